```python
import math
import jax, jax.numpy as jnp
from jax import lax
import numpy as np

D_MODEL = 2048
BATCH = 16
SEQ = 256
DEPTH = 4
DEC_BATCH = 4
DEC_SEQ = 1024
PAST_LEN = 512

GRID_W = 64
RG_WIDTH = 1024
RG_HEADS = 8
RG_BLOCK = RG_WIDTH // RG_HEADS
RG_CONV = 4
RG_C = 8.0
S5_WIDTH = 1024
S5_GROUP_CH = 16
S5_GROUPS = S5_WIDTH // S5_GROUP_CH
S5_STATE = 64
NA_HEADS = 8
NA_HEAD_DIM = 128
NA_WIDTH = NA_HEADS * NA_HEAD_DIM
NA_KH = 8
NA_KW = 16
NA_QB_W = 16
NA_KB_W = 32
NA_SCALE = NA_HEAD_DIM ** -0.5
Q_BLOCK = 128
NEG_INF = -1e30
N_BRANCH = 3
FF_HIDDEN = (8 * D_MODEL + 3 * 256 - 1) // (3 * 256) * 256
RMS_EPS = 1e-6
OFF_RG_X = 0
OFF_RG_G = OFF_RG_X + RG_WIDTH
OFF_S5 = OFF_RG_G + RG_WIDTH
OFF_Q = OFF_S5 + S5_WIDTH
OFF_K = OFF_Q + NA_WIDTH
OFF_V = OFF_K + NA_WIDTH
OFF_GATE = OFF_V + NA_WIDTH
IN_WIDTH = OFF_GATE + N_BRANCH * D_MODEL

kernel_name = "hybrid_rglru_s5_natten_prefix_dit_step"


def rms_norm(x, g):
    xf = x.astype(jnp.float32)
    y = xf * lax.rsqrt(jnp.mean(xf * xf, axis=-1, keepdims=True) + RMS_EPS)
    return (y * g.astype(jnp.float32)).astype(x.dtype)


def modulation(cond, w_mod, b_mod):
    m = jax.nn.silu(cond) @ w_mod + b_mod
    return jnp.split(m, 6, axis=-1)


def linear_scan(a, b, h0, reverse):
    if reverse:
        b = b.at[:, -1].add(a[:, -1] * h0)
    else:
        b = b.at[:, 0].add(a[:, 0] * h0)

    def combine(l, r):
        return (l[0] * r[0], r[0] * l[1] + r[1])

    _, h = lax.associative_scan(combine, (a, b), reverse=reverse, axis=1)
    return h


def depthwise_conv_centred(x, w, b):
    k = w.shape[0]
    pad_l = (k - 1) // 2
    y = lax.conv_general_dilated(x, w[:, None, :], window_strides=(1,), padding=[(pad_l, k - 1 - pad_l)],
                                 dimension_numbers=('NWC', 'WIO', 'NWC'), feature_group_count=x.shape[-1])
    return y + b


def rglru_branch(xr, xg, conv_w, conv_b, gate_w, gate_b, lam, h0, return_state):
    f32 = jnp.float32
    x = depthwise_conv_centred(xr, conv_w, conv_b)
    bsz, length, _ = x.shape
    xh = x.reshape(bsz, length, RG_HEADS, RG_BLOCK)
    gl = jnp.einsum('blhi,dghij->dgblhj', xh, gate_w).reshape(2, 2, bsz, length, RG_WIDTH)
    gates = jax.nn.sigmoid((gl + gate_b[:, :, None, None, :]).astype(f32))
    r, i = gates[:, 0], gates[:, 1]
    log_a = -RG_C * r * jax.nn.softplus(-lam.astype(f32))[:, None, None, :]
    a = jnp.exp(log_a)
    bx = jnp.sqrt(-jnp.expm1(2.0 * log_a)) * i * x.astype(f32)[None]
    h_f = linear_scan(a[0], bx[0], h0[:, 0], reverse=False)
    h_b = linear_scan(a[1], bx[1], h0[:, 1], reverse=True)
    y = ((h_f + h_b) * jax.nn.gelu(xg.astype(f32))).astype(xr.dtype)
    if return_state:
        return y, jnp.stack([h_f[:, -1], h_b[:, 0]], axis=1)
    return y


def s5_branch(u, lam_re, lam_im, log_dt, b_re, b_im, c_re, c_im, d, w_glu, h0, return_state):
    f32 = jnp.float32
    bsz, length, _ = u.shape
    uc = u.astype(f32).reshape(bsz, length, S5_GROUPS, S5_GROUP_CH).astype(jnp.complex64)
    lam = lax.complex(lam_re.astype(f32), lam_im.astype(f32))
    dt = jnp.exp(log_dt.astype(f32))[..., None]
    lam_bar = jnp.exp(lam * dt)
    b_bar = ((lam_bar - 1.0) / lam)[..., None] * lax.complex(b_re.astype(f32), b_im.astype(f32))
    bu = jnp.einsum('blgn,dgpn->dblgp', uc, b_bar)
    h_f = linear_scan(jnp.broadcast_to(lam_bar[0], bu.shape[1:]), bu[0], h0[:, 0], reverse=False)
    h_b = linear_scan(jnp.broadcast_to(lam_bar[1], bu.shape[1:]), bu[1], h0[:, 1], reverse=True)
    c_mat = lax.complex(c_re.astype(f32), c_im.astype(f32))
    y = jnp.real(jnp.einsum('blgp,gnp->blgn', h_f, c_mat[0]) + jnp.einsum('blgp,gnp->blgn', h_b, c_mat[1]))
    y = y.reshape(bsz, length, S5_WIDTH) + d.astype(f32) * u.astype(f32)
    z = jax.nn.gelu(y)
    out = (z * jax.nn.sigmoid(z @ w_glu.astype(f32))).astype(u.dtype)
    if return_state:
        return out, jnp.stack([h_f[:, -1], h_b[:, 0]], axis=1)
    return out


def context_attention(q, k, v):
    bsz, length, h, dh = q.shape
    qb = q.reshape(bsz, length // Q_BLOCK, Q_BLOCK, h, dh).transpose(1, 0, 2, 3, 4)

    def one_block(qi):
        s = jnp.einsum('bqhd,bkhd->bhqk', qi, k).astype(jnp.float32)
        p = jax.nn.softmax(s, axis=-1).astype(v.dtype)
        return jnp.einsum('bhqk,bkhd->bqhd', p, v)

    o = lax.map(one_block, qb)
    return o.transpose(1, 0, 2, 3, 4).reshape(bsz, length, h * dh)


def _na_tables(rows):
    kh = min(NA_KH, rows)
    ncb = GRID_W // NA_QB_W
    r = np.arange(rows)
    row_start = np.clip(r - kh // 2, 0, rows - kh)
    key_rows = row_start[:, None] + np.arange(kh)[None, :]
    blk_start = np.clip(np.arange(ncb) * NA_QB_W - NA_KW // 2, 0, GRID_W - NA_KB_W)
    key_cols = blk_start[:, None] + np.arange(NA_KB_W)[None, :]
    q_cols = np.arange(ncb)[:, None] * NA_QB_W + np.arange(NA_QB_W)[None, :]
    win_start = np.clip(q_cols - NA_KW // 2, 0, GRID_W - NA_KW)
    in_win = (key_cols[:, None, :] >= win_start[:, :, None]) & (key_cols[:, None, :] < win_start[:, :, None] + NA_KW)
    key_idx = (key_rows[:, None, :, None] * GRID_W + key_cols[None, :, None, :]).reshape(rows, ncb, kh * NA_KB_W)
    row_part = (key_rows - r[:, None] + NA_KH - 1) * (2 * NA_KW - 1)
    col_part = np.clip(key_cols[:, None, :] - q_cols[:, :, None], -(NA_KW - 1), NA_KW - 1) + NA_KW - 1
    bias_idx = (row_part[:, None, None, :, None] + col_part[None, :, :, None, :]).reshape(rows, ncb, NA_QB_W, kh * NA_KB_W)
    mask = np.broadcast_to(in_win[None, :, :, None, :], (rows, ncb, NA_QB_W, kh, NA_KB_W)).reshape(rows, ncb, NA_QB_W, kh * NA_KB_W)
    return jnp.asarray(key_idx, jnp.int32), jnp.asarray(bias_idx, jnp.int32), jnp.asarray(mask)


def neighbourhood_attention(q, k, v, k_ctx, v_ctx, rpb, rows):
    bsz, length, h, dh = q.shape
    ncb = GRID_W // NA_QB_W
    key_idx, bias_idx, mask = _na_tables(rows)
    rpb_flat = rpb.reshape(h, -1)
    q_rows = q.reshape(bsz, rows, ncb, NA_QB_W, h, dh).transpose(1, 0, 2, 3, 4, 5)

    def row_step(args):
        q_r, kidx, bidx, msk = args
        k_loc = k[:, kidx]
        v_loc = v[:, kidx]
        s_loc = jnp.einsum('bjqhd,bjkhd->bhjqk', q_r, k_loc).astype(jnp.float32)
        s_loc = jnp.where(msk, s_loc + rpb_flat[:, bidx].astype(jnp.float32), NEG_INF)
        s_ctx = jnp.einsum('bjqhd,bchd->bhjqc', q_r, k_ctx).astype(jnp.float32)
        p = jax.nn.softmax(jnp.concatenate([s_loc, s_ctx], axis=-1), axis=-1).astype(v.dtype)
        kb = kidx.shape[-1]
        return (jnp.einsum('bhjqk,bjkhd->bjqhd', p[..., :kb], v_loc)
                + jnp.einsum('bhjqc,bchd->bjqhd', p[..., kb:], v_ctx))

    o = lax.map(row_step, (q_rows, key_idx, bias_idx, mask))
    return o.transpose(1, 0, 2, 3, 4, 5).reshape(bsz, length, h * dh)


def mixer_proj(x, g_pre, shift, scale, w_in):
    proj = (rms_norm(x, g_pre) * (1 + scale) + shift) @ w_in
    bsz, length, _ = proj.shape
    xr = proj[..., OFF_RG_X:OFF_RG_G]
    xg = proj[..., OFF_RG_G:OFF_S5]
    u = proj[..., OFF_S5:OFF_Q]
    q = proj[..., OFF_Q:OFF_K].reshape(bsz, length, NA_HEADS, NA_HEAD_DIM) * NA_SCALE
    k = proj[..., OFF_K:OFF_V].reshape(bsz, length, NA_HEADS, NA_HEAD_DIM)
    v = proj[..., OFF_V:OFF_GATE].reshape(bsz, length, NA_HEADS, NA_HEAD_DIM)
    return xr, xg, u, q, k, v, proj[..., OFF_GATE:]


def merge_branches(x, gate_logits, y_rg, y_s5, y_na, w_rg_out, w_s5_out, w_na_out, w_o, g_post, gate):
    g = jax.nn.sigmoid(gate_logits.astype(jnp.float32)).astype(x.dtype)
    g_rg, g_s5, g_na = jnp.split(g, N_BRANCH, axis=-1)
    m = g_rg * (y_rg @ w_rg_out) + g_s5 * (y_s5 @ w_s5_out) + g_na * (y_na @ w_na_out)
    return x + gate * rms_norm(m @ w_o, g_post)


def ffn_sublayer(x, g_pre, g_post, shift, scale, gate, w_ffn_in, w_ffn_out):
    h = rms_norm(x, g_pre) * (1 + scale) + shift
    a, b = jnp.split(h @ w_ffn_in, 2, axis=-1)
    return x + gate * rms_norm((jax.nn.silu(a) * b) @ w_ffn_out, g_post)


def setup_inputs(seed: int = 0) -> dict:
    key = jax.random.key(seed)
    ks = iter(jax.random.split(key, 48))
    D = D_MODEL

    def nrm(shape, s):
        return jax.random.normal(next(ks), shape, jnp.float32) * s

    x_prompt = nrm((BATCH, SEQ, D), 1.0)
    x_sample = nrm((DEC_BATCH, DEC_SEQ, D), 1.0)
    cache_na_k = nrm((DEC_BATCH, DEPTH, PAST_LEN, NA_HEADS, NA_HEAD_DIM), 1.0)
    cache_na_v = nrm((DEC_BATCH, DEPTH, PAST_LEN, NA_HEADS, NA_HEAD_DIM), 1.0)
    state_rglru = nrm((DEC_BATCH, DEPTH, 2, RG_WIDTH), 1.0)
    state_s5 = nrm((DEC_BATCH, DEPTH, 2, S5_GROUPS, S5_STATE, 2), 0.1)
    c = nrm((DEC_BATCH, D), 1.0)
    c_ctx = nrm((D,), 1.0)
    w_mod = nrm((DEPTH, D, 6 * D), 0.5 * D ** -0.5)
    b_mod = nrm((DEPTH, 6 * D), 0.02)
    g_mix_pre = 1.0 + nrm((DEPTH, D), 0.02)
    g_mix_post = 1.0 + nrm((DEPTH, D), 0.02)
    g_ffn_pre = 1.0 + nrm((DEPTH, D), 0.02)
    g_ffn_post = 1.0 + nrm((DEPTH, D), 0.02)
    w_in = nrm((DEPTH, D, IN_WIDTH), D ** -0.5)
    rg_conv_w = nrm((DEPTH, RG_CONV, RG_WIDTH), RG_CONV ** -0.5)
    rg_conv_b = nrm((DEPTH, RG_WIDTH), 0.02)
    rg_gate_w = nrm((DEPTH, 2, 2, RG_HEADS, RG_BLOCK, RG_BLOCK), RG_BLOCK ** -0.5)
    rg_gate_b = nrm((DEPTH, 2, 2, RG_WIDTH), 0.02)
    a_pow = jax.random.uniform(next(ks), (DEPTH, 2, RG_WIDTH), jnp.float32, 0.9, 0.999)
    a_base = a_pow ** (1.0 / RG_C)
    rg_lambda = jnp.log(a_base) - jnp.log1p(-a_base)
    s5_lambda_re = -0.5 + nrm((DEPTH, 2, S5_GROUPS, S5_STATE), 0.01)
    s5_lambda_im = math.pi * jnp.arange(S5_STATE, dtype=jnp.float32) + nrm((DEPTH, 2, S5_GROUPS, S5_STATE), 0.01)
    s5_log_dt = jax.random.uniform(next(ks), (DEPTH, 2, S5_GROUPS), jnp.float32, math.log(1e-3), math.log(1e-1))
    s5_b_re = nrm((DEPTH, 2, S5_GROUPS, S5_STATE, S5_GROUP_CH), (2 * S5_GROUP_CH) ** -0.5)
    s5_b_im = nrm((DEPTH, 2, S5_GROUPS, S5_STATE, S5_GROUP_CH), (2 * S5_GROUP_CH) ** -0.5)
    s5_c_re = nrm((DEPTH, 2, S5_GROUPS, S5_GROUP_CH, S5_STATE), S5_STATE ** -0.5)
    s5_c_im = nrm((DEPTH, 2, S5_GROUPS, S5_GROUP_CH, S5_STATE), S5_STATE ** -0.5)
    s5_d = nrm((DEPTH, S5_WIDTH), 1.0)
    s5_w_glu = nrm((DEPTH, S5_WIDTH, S5_WIDTH), S5_WIDTH ** -0.5)
    na_rpb = nrm((DEPTH, NA_HEADS, 2 * NA_KH - 1, 2 * NA_KW - 1), 0.02)
    w_rg_out = nrm((DEPTH, RG_WIDTH, D), RG_WIDTH ** -0.5)
    w_s5_out = nrm((DEPTH, S5_WIDTH, D), S5_WIDTH ** -0.5)
    w_na_out = nrm((DEPTH, NA_WIDTH, D), NA_WIDTH ** -0.5)
    w_o = nrm((DEPTH, D, D), D ** -0.5)
    w_ffn_in = nrm((DEPTH, D, 2 * FF_HIDDEN), D ** -0.5)
    w_ffn_out = nrm((DEPTH, FF_HIDDEN, D), FF_HIDDEN ** -0.5)
    return {"x_prompt": x_prompt, "x_sample": x_sample, "cache_na_k": cache_na_k, "cache_na_v": cache_na_v,
            "state_rglru": state_rglru, "state_s5": state_s5, "c": c, "c_ctx": c_ctx,
            "w_mod": w_mod, "b_mod": b_mod, "g_mix_pre": g_mix_pre, "g_mix_post": g_mix_post,
            "g_ffn_pre": g_ffn_pre, "g_ffn_post": g_ffn_post, "w_in": w_in,
            "rg_conv_w": rg_conv_w, "rg_conv_b": rg_conv_b, "rg_gate_w": rg_gate_w, "rg_gate_b": rg_gate_b,
            "rg_lambda": rg_lambda, "s5_lambda_re": s5_lambda_re, "s5_lambda_im": s5_lambda_im,
            "s5_log_dt": s5_log_dt, "s5_b_re": s5_b_re, "s5_b_im": s5_b_im, "s5_c_re": s5_c_re,
            "s5_c_im": s5_c_im, "s5_d": s5_d, "s5_w_glu": s5_w_glu, "na_rpb": na_rpb,
            "w_rg_out": w_rg_out, "w_s5_out": w_s5_out, "w_na_out": w_na_out, "w_o": w_o,
            "w_ffn_in": w_ffn_in, "w_ffn_out": w_ffn_out}


def reference(x_prompt, x_sample, cache_na_k, cache_na_v, state_rglru, state_s5, c, c_ctx,
              w_mod, b_mod, g_mix_pre, g_mix_post, g_ffn_pre, g_ffn_post, w_in,
              rg_conv_w, rg_conv_b, rg_gate_w, rg_gate_b, rg_lambda,
              s5_lambda_re, s5_lambda_im, s5_log_dt, s5_b_re, s5_b_im, s5_c_re, s5_c_im, s5_d, s5_w_glu,
              na_rpb, w_rg_out, w_s5_out, w_na_out, w_o, w_ffn_in, w_ffn_out):
    f32 = jnp.float32
    rows = x_sample.shape[1] // GRID_W
    bp = x_prompt.shape[0]
    s5_cache = lax.complex(state_s5[..., 0].astype(f32), state_s5[..., 1].astype(f32))
    rg_zero = jnp.zeros((bp, 2, RG_WIDTH), f32)
    s5_zero = jnp.zeros((bp, 2, S5_GROUPS, S5_STATE), jnp.complex64)
    xp, xs = x_prompt, x_sample
    ks, vs, rgs, s5s = [], [], [], []
    for l in range(DEPTH):
        rg_p = (rg_conv_w[l], rg_conv_b[l], rg_gate_w[l], rg_gate_b[l], rg_lambda[l])
        s5_p = (s5_lambda_re[l], s5_lambda_im[l], s5_log_dt[l], s5_b_re[l], s5_b_im[l],
                s5_c_re[l], s5_c_im[l], s5_d[l], s5_w_glu[l])
        out_p = (w_rg_out[l], w_s5_out[l], w_na_out[l], w_o[l])

        mp = modulation(c_ctx[None, None, :], w_mod[l], b_mod[l])
        xr, xg, u, q, k, v, gl = mixer_proj(xp, g_mix_pre[l], mp[0], mp[1], w_in[l])
        y_rg, st_rg = rglru_branch(xr, xg, *rg_p, rg_zero, True)
        y_s5, st_s5 = s5_branch(u, *s5_p, s5_zero, True)
        y_na = context_attention(q, k, v)
        xp = merge_branches(xp, gl, y_rg, y_s5, y_na, *out_p, g_mix_post[l], mp[2])
        xp = ffn_sublayer(xp, g_ffn_pre[l], g_ffn_post[l], mp[3], mp[4], mp[5], w_ffn_in[l], w_ffn_out[l])
        ks.append(k)
        vs.append(v)
        rgs.append(st_rg)
        s5s.append(jnp.stack([jnp.real(st_s5), jnp.imag(st_s5)], axis=-1))

        ms = modulation(c[:, None, :], w_mod[l], b_mod[l])
        xr, xg, u, q, k, v, gl = mixer_proj(xs, g_mix_pre[l], ms[0], ms[1], w_in[l])
        y_rg = rglru_branch(xr, xg, *rg_p, state_rglru[:, l], False)
        y_s5 = s5_branch(u, *s5_p, s5_cache[:, l], False)
        y_na = neighbourhood_attention(q, k, v, cache_na_k[:, l], cache_na_v[:, l], na_rpb[l], rows)
        xs = merge_branches(xs, gl, y_rg, y_s5, y_na, *out_p, g_mix_post[l], ms[2])
        xs = ffn_sublayer(xs, g_ffn_pre[l], g_ffn_post[l], ms[3], ms[4], ms[5], w_ffn_in[l], w_ffn_out[l])

    new_na_k = jnp.stack(ks, axis=1)
    new_na_v = jnp.stack(vs, axis=1)
    new_rglru = jnp.stack(rgs, axis=1)
    new_s5 = jnp.stack(s5s, axis=1)
    return (xp, xs, new_na_k, new_na_v, new_rglru, new_s5)
```

```python
import functools
import math

import numpy as np
import jax
import jax.numpy as jnp
from jax import lax
from jax.experimental import pallas as pl
from jax.experimental.pallas import tpu as pltpu

F32 = jnp.float32
BF16 = jnp.bfloat16

D_MODEL = 2048
BATCH = 16
SEQ = 256
DEPTH = 4
DEC_BATCH = 4
DEC_SEQ = 1024
PAST_LEN = 512
GRID_W = 64
RG_WIDTH = 1024
RG_HEADS = 8
RG_BLOCK = RG_WIDTH // RG_HEADS
RG_CONV = 4
RG_C = 8.0
S5_WIDTH = 1024
S5_GROUP_CH = 16
S5_GROUPS = S5_WIDTH // S5_GROUP_CH
S5_STATE = 64
NA_HEADS = 8
NA_HEAD_DIM = 128
NA_WIDTH = NA_HEADS * NA_HEAD_DIM
NA_KH = 8
NA_KW = 16
NA_SCALE = NA_HEAD_DIM ** -0.5
NEG_INF = -1e30
N_BRANCH = 3
FF_HIDDEN = (8 * D_MODEL + 3 * 256 - 1) // (3 * 256) * 256
RMS_EPS = 1e-6
OFF_RG_X = 0
OFF_RG_G = OFF_RG_X + RG_WIDTH
OFF_S5 = OFF_RG_G + RG_WIDTH
OFF_Q = OFF_S5 + S5_WIDTH
OFF_K = OFF_Q + NA_WIDTH
OFF_V = OFF_K + NA_WIDTH
OFF_GATE = OFF_V + NA_WIDTH
IN_WIDTH = OFF_GATE + N_BRANCH * D_MODEL

N_CTX_TOK = BATCH * SEQ
N_LAT_TOK = DEC_BATCH * DEC_SEQ
N_TOK = N_CTX_TOK + N_LAT_TOK
N_COND = 8
GRID_ROWS = DEC_SEQ // GRID_W
S5_CB = 128
S5_CB_GROUPS = S5_CB // S5_GROUP_CH
S5_CB_STATE = S5_CB_GROUPS * S5_STATE
N_S5_CB = S5_WIDTH // S5_CB
SUBLANES = 8
S5_ROWS = 256

VMEM_LIMIT = 52 * 1024 * 1024


def _cparams(n_grid):
    return pltpu.CompilerParams(dimension_semantics=("arbitrary",) * n_grid, vmem_limit_bytes=VMEM_LIMIT)


def _cond_row(start_row):
    return jnp.where(start_row < N_CTX_TOK, 0, 1 + (start_row - N_CTX_TOK) // DEC_SEQ)


def _mod_index(layer, k, start_row):
    return (layer * 6 + k) * N_COND + _cond_row(start_row)


def _rms(x):
    return x * lax.rsqrt(jnp.mean(x * x, axis=-1, keepdims=True) + RMS_EPS)


def _gelu_tanh(x):
    return 0.5 * x * (1.0 + jnp.tanh(math.sqrt(2.0 / math.pi) * (x + 0.044715 * (x * x * x))))


def _mod_kernel(c_ref, w_ref, b_ref, o_ref):
    cond = c_ref[...]
    a = (cond * jax.nn.sigmoid(cond)).astype(BF16)
    o_ref[...] = jnp.dot(a, w_ref[...].astype(BF16), preferred_element_type=F32) + b_ref[...]


def _modulation(cond, w_mod, b_mod):
    tn = 1024
    n_out = 6 * D_MODEL
    return pl.pallas_call(
        _mod_kernel,
        grid=(DEPTH, n_out // tn),
        in_specs=[
            pl.BlockSpec((N_COND, D_MODEL), lambda l, n: (0, 0)),
            pl.BlockSpec((None, D_MODEL, tn), lambda l, n: (l, 0, n)),
            pl.BlockSpec((None, 1, tn), lambda l, n: (l, 0, n)),
        ],
        out_specs=pl.BlockSpec((None, N_COND, tn), lambda l, n: (l, 0, n)),
        out_shape=jax.ShapeDtypeStruct((DEPTH, N_COND, n_out), F32),
        compiler_params=_cparams(2),
        name="modulation",
    )(cond, w_mod, b_mod.reshape(DEPTH, 1, n_out))


def _prenorm_kernel(x_ref, g_ref, sc_ref, sh_ref, o_ref):
    y = _rms(x_ref[...]) * g_ref[...]
    o_ref[...] = (y * (1.0 + sc_ref[...]) + sh_ref[...]).astype(BF16)


def _prenorm(x, g_all, modv, layer, k_shift, k_scale):
    tm = 512
    return pl.pallas_call(
        _prenorm_kernel,
        grid=(N_TOK // tm,),
        in_specs=[
            pl.BlockSpec((tm, D_MODEL), lambda i: (i, 0)),
            pl.BlockSpec((None, 1, D_MODEL), lambda i: (layer, 0, 0)),
            pl.BlockSpec((None, 1, D_MODEL), lambda i: (_mod_index(layer, k_scale, i * tm), 0, 0)),
            pl.BlockSpec((None, 1, D_MODEL), lambda i: (_mod_index(layer, k_shift, i * tm), 0, 0)),
        ],
        out_specs=pl.BlockSpec((tm, D_MODEL), lambda i: (i, 0)),
        out_shape=jax.ShapeDtypeStruct((N_TOK, D_MODEL), BF16),
        compiler_params=_cparams(1),
        name="prenorm",
    )(x, g_all, modv, modv)


def _proj_kernel(a_ref, w_ref, o_ref, wb_ref):
    @pl.when(pl.program_id(1) == 0)
    def _():
        wb_ref[...] = w_ref[...].astype(BF16)

    o_ref[...] = jnp.dot(a_ref[...], wb_ref[...], preferred_element_type=F32)


def _in_proj(h, w_in, layer):
    tm, tn = 1024, 1024
    return pl.pallas_call(
        _proj_kernel,
        grid=(IN_WIDTH // tn, N_TOK // tm),
        in_specs=[
            pl.BlockSpec((tm, D_MODEL), lambda n, m: (m, 0)),
            pl.BlockSpec((None, D_MODEL, tn), lambda n, m: (layer, 0, n)),
        ],
        out_specs=pl.BlockSpec((tm, tn), lambda n, m: (m, n)),
        out_shape=jax.ShapeDtypeStruct((N_TOK, IN_WIDTH), F32),
        scratch_shapes=[pltpu.VMEM((D_MODEL, tn), BF16)],
        compiler_params=_cparams(2),
        name="in_proj",
    )(h, w_in)


def _tile_scan(a, b, carry, reverse):
    row = lax.broadcasted_iota(jnp.int32, a.shape, 0)
    for d in (1, 2, 4):
        if reverse:
            keep = row < SUBLANES - d
            shift = SUBLANES - d
        else:
            keep = row >= d
            shift = d
        sa = jnp.where(keep, pltpu.roll(a, shift, 0), 1.0)
        sb = jnp.where(keep, pltpu.roll(b, shift, 0), 0.0)
        b = a * sb + b
        a = a * sa
    h = a * carry + b
    return h, (h[0:1] if reverse else h[SUBLANES - 1:SUBLANES])


RG_ROWS = 128


def _rg_kernel(xr_ref, xg_ref, cw_ref, cb_ref, gw_ref, gb_ref, lam_ref, h0_ref, y_ref, st_ref,
               xp_ref, a_ref, b_ref, hf_ref, *, length, heads):
    pad = SUBLANES
    win = RG_ROWS + 2 * pad
    zeros = jnp.zeros((pad, heads * RG_BLOCK), F32)
    xp_ref[0:pad, :] = zeros
    xp_ref[pad + length:pad + length + pad, :] = zeros
    xp_ref[pad:pad + length, :] = xr_ref[...]
    neg = -lam_ref[...]
    softplus = jnp.maximum(neg, 0.0) + jnp.log1p(jnp.exp(-jnp.abs(neg)))

    def gates(c, _):
        r0 = pl.multiple_of(c * RG_ROWS, RG_ROWS)
        w = xp_ref[pl.ds(r0, win), :]
        tap = lambda s: pltpu.roll(w, s % win, 0)[pad:pad + RG_ROWS]
        x = (cw_ref[1:2, :] * w[pad:pad + RG_ROWS] + cb_ref[...] + cw_ref[0:1, :] * tap(1)
             + cw_ref[2:3, :] * tap(-1) + cw_ref[3:4, :] * tap(-2))
        xb = x.astype(BF16)
        for j in range(heads):
            cs = slice(j * RG_BLOCK, (j + 1) * RG_BLOCK)
            gl = jnp.dot(xb[:, cs], gw_ref[j], preferred_element_type=F32)
            for d in range(2):
                r = jax.nn.sigmoid(gl[:, (2 * d) * RG_BLOCK:(2 * d + 1) * RG_BLOCK] + gb_ref[2 * d:2 * d + 1, cs])
                i = jax.nn.sigmoid(gl[:, (2 * d + 1) * RG_BLOCK:(2 * d + 2) * RG_BLOCK]
                                   + gb_ref[2 * d + 1:2 * d + 2, cs])
                log_a = (-RG_C) * r * softplus[d:d + 1, cs]
                t = jnp.tanh(log_a)
                a_ref[d, pl.ds(r0, RG_ROWS), cs] = jnp.exp(log_a)
                b_ref[d, pl.ds(r0, RG_ROWS), cs] = jnp.sqrt(-2.0 * t / (1.0 - t)) * i * x[:, cs]
        return 0

    lax.fori_loop(0, length // RG_ROWS, gates, 0)
    n_tiles = length // SUBLANES

    def fwd(k, carry):
        r0 = pl.multiple_of(k * SUBLANES, SUBLANES)
        h, carry = _tile_scan(a_ref[0, pl.ds(r0, SUBLANES), :], b_ref[0, pl.ds(r0, SUBLANES), :], carry, False)
        hf_ref[pl.ds(r0, SUBLANES), :] = h
        return carry

    def bwd(k, carry):
        r0 = pl.multiple_of((n_tiles - 1 - k) * SUBLANES, SUBLANES)
        h, carry = _tile_scan(a_ref[1, pl.ds(r0, SUBLANES), :], b_ref[1, pl.ds(r0, SUBLANES), :], carry, True)
        y = (hf_ref[pl.ds(r0, SUBLANES), :] + h) * _gelu_tanh(xg_ref[pl.ds(r0, SUBLANES), :])
        y_ref[pl.ds(r0, SUBLANES), :] = y.astype(BF16)
        return carry

    st_ref[0:1, :] = lax.fori_loop(0, n_tiles, fwd, h0_ref[0:1, :])
    st_ref[1:2, :] = lax.fori_loop(0, n_tiles, bwd, h0_ref[1:2, :])


def _rglru(proj, cw, cb, gw, gb, lam, h0, *, length, n_seq, row0, heads):
    width = heads * RG_BLOCK
    n_cb = RG_WIDTH // width
    rb0 = row0 // length
    y, st = pl.pallas_call(
        functools.partial(_rg_kernel, length=length, heads=heads),
        grid=(n_seq, n_cb),
        in_specs=[
            pl.BlockSpec((length, width), lambda s, c: (rb0 + s, OFF_RG_X // width + c)),
            pl.BlockSpec((length, width), lambda s, c: (rb0 + s, OFF_RG_G // width + c)),
            pl.BlockSpec((RG_CONV, width), lambda s, c: (0, c)),
            pl.BlockSpec((1, width), lambda s, c: (0, c)),
            pl.BlockSpec((heads, RG_BLOCK, 4 * RG_BLOCK), lambda s, c: (c, 0, 0)),
            pl.BlockSpec((4, width), lambda s, c: (0, c)),
            pl.BlockSpec((2, width), lambda s, c: (0, c)),
            pl.BlockSpec((None, 2, width), lambda s, c: (s, 0, c)),
        ],
        out_specs=[
            pl.BlockSpec((length, width), lambda s, c: (s, c)),
            pl.BlockSpec((None, 2, width), lambda s, c: (s, 0, c)),
        ],
        out_shape=[
            jax.ShapeDtypeStruct((n_seq * length, RG_WIDTH), BF16),
            jax.ShapeDtypeStruct((n_seq, 2, RG_WIDTH), F32),
        ],
        scratch_shapes=[
            pltpu.VMEM((length + 2 * SUBLANES, width), F32),
            pltpu.VMEM((2, length, width), F32),
            pltpu.VMEM((2, length, width), F32),
            pltpu.VMEM((length, width), F32),
        ],
        compiler_params=_cparams(2),
        name=f"rglru_{length}",
    )(proj, proj, cw, cb, gw, gb, lam, h0)
    return y, st


def _cmul(ar, ai, br, bi):
    return ar * br - ai * bi, ar * bi + ai * br


def _s5_kernel(u_ref, bm_ref, lam_ref, cm_ref, d_ref, h0_ref, z_ref, st_ref, up_ref, bu_ref, *, length):
    ns = S5_CB_STATE
    chunk = length // SUBLANES
    for t in range(chunk):
        up_ref[t * SUBLANES:(t + 1) * SUBLANES, :] = u_ref[pl.ds(t, SUBLANES, stride=chunk), :]

    def b_proj(c, _):
        r0 = pl.multiple_of(c * S5_ROWS, S5_ROWS)
        bu_ref[pl.ds(r0, S5_ROWS), :] = jnp.dot(up_ref[pl.ds(r0, S5_ROWS), :].astype(BF16), bm_ref[...],
                                                preferred_element_type=F32)
        return 0

    lax.fori_loop(0, length // S5_ROWS, b_proj, 0)

    lam_pow = []
    for d in range(2):
        pr, pi = lam_ref[2 * d:2 * d + 1, :], lam_ref[2 * d + 1:2 * d + 2, :]
        for _ in range(int(math.log2(chunk))):
            pr, pi = _cmul(pr, pi, pr, pi)
        lam_pow.append((pr, pi))

    def tile(t, d):
        r0 = pl.multiple_of(t * SUBLANES, SUBLANES)
        return (bu_ref.at[pl.ds(r0, SUBLANES), (2 * d) * ns:(2 * d + 1) * ns],
                bu_ref.at[pl.ds(r0, SUBLANES), (2 * d + 1) * ns:(2 * d + 2) * ns])

    def step(t, d, hr, hi, lr, li):
        xr_ref, xi_ref = tile(t, d)
        nr, ni = _cmul(lr, li, hr, hi)
        return nr + xr_ref[...], ni + xi_ref[...]

    lam_b = [tuple(jnp.broadcast_to(lam_ref[k:k + 1, :], (SUBLANES, ns)) for k in (2 * d, 2 * d + 1)) for d in range(2)]
    zero = jnp.zeros((SUBLANES, ns), F32)

    def local_ends(t, c):
        fr, fi, br, bi = c
        fr, fi = step(t, 0, fr, fi, *lam_b[0])
        br, bi = step(chunk - 1 - t, 1, br, bi, *lam_b[1])
        return fr, fi, br, bi

    efr, efi, ebr, ebi = lax.fori_loop(0, chunk, local_ends, (zero, zero, zero, zero))

    sr, si = h0_ref[0:1, :], h0_ref[1:2, :]
    f_rows = []
    for j in range(SUBLANES):
        f_rows.append((sr, si))
        nr, ni = _cmul(*lam_pow[0], sr, si)
        sr, si = nr + efr[j:j + 1], ni + efi[j:j + 1]
    st_ref[0:1, :], st_ref[1:2, :] = sr, si
    sr, si = h0_ref[2:3, :], h0_ref[3:4, :]
    b_rows = [None] * SUBLANES
    for j in reversed(range(SUBLANES)):
        b_rows[j] = (sr, si)
        nr, ni = _cmul(*lam_pow[1], sr, si)
        sr, si = nr + ebr[j:j + 1], ni + ebi[j:j + 1]
    st_ref[2:3, :], st_ref[3:4, :] = sr, si
    start = tuple(jnp.concatenate([rows[j][k] for j in range(SUBLANES)], axis=0)
                  for rows in (f_rows, b_rows) for k in range(2))

    def scan(t, c):
        fr, fi, br, bi = c
        fr, fi = step(t, 0, fr, fi, *lam_b[0])
        xr_ref, xi_ref = tile(t, 0)
        xr_ref[...], xi_ref[...] = fr, fi
        tb = chunk - 1 - t
        br, bi = step(tb, 1, br, bi, *lam_b[1])
        xr_ref, xi_ref = tile(tb, 1)
        xr_ref[...], xi_ref[...] = br, bi
        return fr, fi, br, bi

    lax.fori_loop(0, chunk, scan, start)

    def c_proj(c, _):
        r0 = pl.multiple_of(c * S5_ROWS, S5_ROWS)
        y = jnp.dot(bu_ref[pl.ds(r0, S5_ROWS), :].astype(BF16), cm_ref[...], preferred_element_type=F32)
        up_ref[pl.ds(r0, S5_ROWS), :] = _gelu_tanh(y + d_ref[...] * up_ref[pl.ds(r0, S5_ROWS), :])
        return 0

    lax.fori_loop(0, length // S5_ROWS, c_proj, 0)
    for t in range(chunk):
        z_ref[pl.ds(t, SUBLANES, stride=chunk), :] = up_ref[t * SUBLANES:(t + 1) * SUBLANES, :]


def _s5(proj, bmat, lamrow, cmat, dvec, h0, *, length, n_seq, row0):
    rb0 = row0 // length
    ns = S5_CB_STATE
    return pl.pallas_call(
        functools.partial(_s5_kernel, length=length),
        grid=(N_S5_CB, n_seq),
        in_specs=[
            pl.BlockSpec((length, S5_CB), lambda c, s: (rb0 + s, OFF_S5 // S5_CB + c)),
            pl.BlockSpec((None, S5_CB, 4 * ns), lambda c, s: (c, 0, 0)),
            pl.BlockSpec((None, 4, ns), lambda c, s: (c, 0, 0)),
            pl.BlockSpec((None, 4 * ns, S5_CB), lambda c, s: (c, 0, 0)),
            pl.BlockSpec((1, S5_CB), lambda c, s: (0, c)),
            pl.BlockSpec((None, None, 4, ns), lambda c, s: (s, c, 0, 0)),
        ],
        out_specs=[
            pl.BlockSpec((length, S5_CB), lambda c, s: (s, c)),
            pl.BlockSpec((None, None, 4, ns), lambda c, s: (s, c, 0, 0)),
        ],
        out_shape=[
            jax.ShapeDtypeStruct((n_seq * length, S5_WIDTH), F32),
            jax.ShapeDtypeStruct((n_seq, N_S5_CB, 4, ns), F32),
        ],
        scratch_shapes=[
            pltpu.VMEM((length, S5_CB), F32),
            pltpu.VMEM((length, 4 * ns), F32),
        ],
        compiler_params=_cparams(2),
        name=f"s5_{length}",
    )(proj, bmat, lamrow, cmat, dvec, h0)


def _s5_tables(lam_re, lam_im, log_dt, b_re, b_im, c_re, c_im):
    lam = lax.complex(lam_re, lam_im)
    dt = jnp.exp(log_dt)[..., None]
    lam_bar = jnp.exp(lam * dt)
    b_bar = ((lam_bar - 1.0) / lam)[..., None] * lax.complex(b_re, b_im)
    eye = jnp.eye(S5_CB_GROUPS, dtype=F32)
    g6 = (2, 2, N_S5_CB, S5_CB_GROUPS)
    bb = jnp.stack([jnp.real(b_bar), jnp.imag(b_bar)], axis=1).reshape(*g6, S5_STATE, S5_GROUP_CH)
    bmat = jnp.einsum('dckgpn,gh->kgndchp', bb, eye).reshape(N_S5_CB, S5_CB, 4 * S5_CB_STATE)
    cc = jnp.stack([c_re, -c_im], axis=1).reshape(*g6, S5_GROUP_CH, S5_STATE)
    cmat = jnp.einsum('dckgnp,gh->kdchpgn', cc, eye).reshape(N_S5_CB, 4 * S5_CB_STATE, S5_CB)
    lamrow = jnp.stack([jnp.real(lam_bar), jnp.imag(lam_bar)], axis=1)
    lamrow = lamrow.reshape(2, 2, N_S5_CB, S5_CB_STATE).transpose(2, 0, 1, 3).reshape(N_S5_CB, 4, S5_CB_STATE)
    return bmat.astype(BF16), lamrow, cmat.astype(BF16)


def _s5_glu_kernel(z_ref, w_ref, o_ref):
    z = z_ref[...]
    g = jnp.dot(z.astype(BF16), w_ref[...].astype(BF16), preferred_element_type=F32)
    o_ref[...] = (z * jax.nn.sigmoid(g)).astype(BF16)


def _s5_glu(z, w_glu, layer):
    tm = 1024
    return pl.pallas_call(
        _s5_glu_kernel,
        grid=(N_TOK // tm,),
        in_specs=[
            pl.BlockSpec((tm, S5_WIDTH), lambda i: (i, 0)),
            pl.BlockSpec((None, S5_WIDTH, S5_WIDTH), lambda i: (layer, 0, 0)),
        ],
        out_specs=pl.BlockSpec((tm, S5_WIDTH), lambda i: (i, 0)),
        out_shape=jax.ShapeDtypeStruct((N_TOK, S5_WIDTH), BF16),
        compiler_params=_cparams(1),
        name="s5_glu",
    )(z, w_glu)


def _dot_nt(a, b):
    return lax.dot_general(a, b, (((1,), (1,)), ((), ())), preferred_element_type=F32)


def _ctx_attn_kernel(q_ref, k_ref, v_ref, o_ref):
    for h in range(NA_HEADS):
        cs = slice(h * NA_HEAD_DIM, (h + 1) * NA_HEAD_DIM)
        q = (q_ref[:, cs] * NA_SCALE).astype(BF16)
        s = _dot_nt(q, k_ref[:, cs].astype(BF16))
        p = jnp.exp(s - jnp.max(s, axis=-1, keepdims=True))
        p = (p / jnp.sum(p, axis=-1, keepdims=True)).astype(BF16)
        o_ref[:, cs] = jnp.dot(p, v_ref[:, cs].astype(BF16), preferred_element_type=F32).astype(BF16)


def _ctx_attention(proj):
    spec = lambda off: pl.BlockSpec((SEQ, NA_WIDTH), lambda b: (b, off // NA_WIDTH))
    return pl.pallas_call(
        _ctx_attn_kernel,
        grid=(BATCH,),
        in_specs=[spec(OFF_Q), spec(OFF_K), spec(OFF_V)],
        out_specs=pl.BlockSpec((SEQ, NA_WIDTH), lambda b: (b, 0)),
        out_shape=jax.ShapeDtypeStruct((N_CTX_TOK, NA_WIDTH), BF16),
        compiler_params=_cparams(1),
        name="ctx_attention",
    )(proj, proj, proj)


def _na_row_start(r):
    return min(max(r - NA_KH // 2, 0), GRID_ROWS - NA_KH)


def _na_variant(r):
    lo, hi = NA_KH // 2, GRID_ROWS - NA_KH // 2
    if r < lo:
        return r
    if r <= hi:
        return lo
    return r - hi + lo


N_NA_VARIANTS = _na_variant(GRID_ROWS - 1) + 1
NA_LOCAL = NA_KH * GRID_W


def _na_kernel(q_ref, k_ref, v_ref, kc_ref, vc_ref, bias_ref, o_ref):
    k = k_ref[...].astype(BF16)
    v = v_ref[...].astype(BF16)
    kc = kc_ref[...].astype(BF16)
    vc = vc_ref[...].astype(BF16)
    for r in range(GRID_ROWS):
        k0 = _na_row_start(r) * GRID_W
        q = (q_ref[r * GRID_W:(r + 1) * GRID_W, :] * NA_SCALE).astype(BF16)
        s_loc = _dot_nt(q, k[k0:k0 + NA_LOCAL]) + bias_ref[_na_variant(r)]
        s_ctx = _dot_nt(q, kc)
        m = jnp.maximum(jnp.max(s_loc, axis=-1, keepdims=True), jnp.max(s_ctx, axis=-1, keepdims=True))
        p_loc = jnp.exp(s_loc - m)
        p_ctx = jnp.exp(s_ctx - m)
        inv = 1.0 / (jnp.sum(p_loc, axis=-1, keepdims=True) + jnp.sum(p_ctx, axis=-1, keepdims=True))
        o = (jnp.dot((p_loc * inv).astype(BF16), v[k0:k0 + NA_LOCAL], preferred_element_type=F32)
             + jnp.dot((p_ctx * inv).astype(BF16), vc, preferred_element_type=F32))
        o_ref[r * GRID_W:(r + 1) * GRID_W, :] = o.astype(BF16)


def _na_attention(proj, cache_k, cache_v, bias, layer):
    rb0 = N_CTX_TOK // DEC_SEQ
    spec = lambda off: pl.BlockSpec((DEC_SEQ, NA_HEAD_DIM), lambda b, h: (rb0 + b, off // NA_HEAD_DIM + h))
    cspec = pl.BlockSpec((None, None, PAST_LEN, NA_HEAD_DIM), lambda b, h: (b, layer, 0, h))
    return pl.pallas_call(
        _na_kernel,
        grid=(DEC_BATCH, NA_HEADS),
        in_specs=[spec(OFF_Q), spec(OFF_K), spec(OFF_V), cspec, cspec,
                  pl.BlockSpec((None, None, N_NA_VARIANTS, GRID_W, NA_LOCAL), lambda b, h: (layer, h, 0, 0, 0))],
        out_specs=pl.BlockSpec((DEC_SEQ, NA_HEAD_DIM), lambda b, h: (b, h)),
        out_shape=jax.ShapeDtypeStruct((N_LAT_TOK, NA_WIDTH), BF16),
        compiler_params=_cparams(2),
        name="na_attention",
    )(proj, proj, proj, cache_k, cache_v, bias)


def _na_bias_tables(na_rpb):
    qc = np.arange(GRID_W)[:, None]
    kc = np.arange(GRID_W)[None, :]
    win = np.clip(qc - NA_KW // 2, 0, GRID_W - NA_KW)
    in_win = (kc >= win) & (kc < win + NA_KW)
    col = np.clip(kc - qc, -(NA_KW - 1), NA_KW - 1) + NA_KW - 1
    toep = jnp.where(in_win, na_rpb[:, :, :, col], NEG_INF)
    rows = [next(r for r in range(GRID_ROWS) if _na_variant(r) == v) for v in range(N_NA_VARIANTS)]
    ridx = np.array([[_na_row_start(r) + i - r + NA_KH - 1 for i in range(NA_KH)] for r in rows])
    tab = toep[:, :, ridx]
    return tab.transpose(0, 1, 2, 4, 3, 5).reshape(DEPTH, NA_HEADS, N_NA_VARIANTS, GRID_W, NA_LOCAL)


def _merge_kernel(y0_ref, y1_ref, y2_ref, w0_ref, w1_ref, w2_ref, g0_ref, g1_ref, g2_ref, o_ref, wb_ref):
    @pl.when(pl.program_id(1) == 0)
    def _():
        wb_ref[0] = w0_ref[...].astype(BF16)
        wb_ref[1] = w1_ref[...].astype(BF16)
        wb_ref[2] = w2_ref[...].astype(BF16)

    m = jax.nn.sigmoid(g0_ref[...]) * jnp.dot(y0_ref[...], wb_ref[0], preferred_element_type=F32)
    m = m + jax.nn.sigmoid(g1_ref[...]) * jnp.dot(y1_ref[...], wb_ref[1], preferred_element_type=F32)
    m = m + jax.nn.sigmoid(g2_ref[...]) * jnp.dot(y2_ref[...], wb_ref[2], preferred_element_type=F32)
    o_ref[...] = m.astype(BF16)


def _merge(y_rg, y_s5, y_na, w_rg, w_s5, w_na, proj, layer):
    tm, tn = 512, 512
    width = RG_WIDTH
    yspec = pl.BlockSpec((tm, width), lambda n, m: (m, 0))
    wspec = pl.BlockSpec((None, width, tn), lambda n, m: (layer, 0, n))
    gspec = lambda b: pl.BlockSpec((tm, tn), lambda n, m: (m, (OFF_GATE + b * D_MODEL) // tn + n))
    return pl.pallas_call(
        _merge_kernel,
        grid=(D_MODEL // tn, N_TOK // tm),
        in_specs=[yspec, yspec, yspec, wspec, wspec, wspec, gspec(0), gspec(1), gspec(2)],
        out_specs=pl.BlockSpec((tm, tn), lambda n, m: (m, n)),
        out_shape=jax.ShapeDtypeStruct((N_TOK, D_MODEL), BF16),
        scratch_shapes=[pltpu.VMEM((N_BRANCH, width, tn), BF16)],
        compiler_params=_cparams(2),
        name="merge",
    )(y_rg, y_s5, y_na, w_rg, w_s5, w_na, proj, proj, proj)


def _res_kernel(*refs, n_k, emit_next):
    if emit_next:
        a_ref, w_ref, x_ref, gp_ref, gate_ref, gn_ref, sc_ref, sh_ref, xo_ref, ho_ref = refs
    else:
        a_ref, w_ref, x_ref, gp_ref, gate_ref, xo_ref = refs
    k = pl.program_id(1)
    part = jnp.dot(a_ref[...], w_ref[...].astype(BF16), preferred_element_type=F32)

    @pl.when(k == 0)
    def _():
        xo_ref[...] = part

    @pl.when(k > 0)
    def _():
        xo_ref[...] += part

    @pl.when(k == n_k - 1)
    def _():
        xn = x_ref[...] + gate_ref[...] * (_rms(xo_ref[...]) * gp_ref[...])
        xo_ref[...] = xn
        if emit_next:
            ho_ref[...] = ((_rms(xn) * gn_ref[...]) * (1.0 + sc_ref[...]) + sh_ref[...]).astype(BF16)


def _out_residual(a, w, x, g_post, modv, layer, k_gate, nxt):
    tm, tk = 512, 512
    k_dim = a.shape[1]
    n_k = k_dim // tk
    row = lambda lyr: pl.BlockSpec((None, 1, D_MODEL), lambda i, k: (lyr, 0, 0))
    modrow = lambda lyr, kk: pl.BlockSpec((None, 1, D_MODEL), lambda i, k: (_mod_index(lyr, kk, i * tm), 0, 0))
    xspec = pl.BlockSpec((tm, D_MODEL), lambda i, k: (i, 0))
    in_specs = [
        pl.BlockSpec((tm, tk), lambda i, k: (i, k)),
        pl.BlockSpec((None, tk, D_MODEL), lambda i, k: (layer, k, 0)),
        xspec, row(layer), modrow(layer, k_gate),
    ]
    args = [a, w, x, g_post, modv]
    out_specs = [xspec]
    out_shape = [jax.ShapeDtypeStruct((N_TOK, D_MODEL), F32)]
    if nxt is not None:
        g_next, l_next, k_shift, k_scale = nxt
        in_specs += [row(l_next), modrow(l_next, k_scale), modrow(l_next, k_shift)]
        args += [g_next, modv, modv]
        out_specs.append(xspec)
        out_shape.append(jax.ShapeDtypeStruct((N_TOK, D_MODEL), BF16))
    out = pl.pallas_call(
        functools.partial(_res_kernel, n_k=n_k, emit_next=nxt is not None),
        grid=(N_TOK // tm, n_k),
        in_specs=in_specs,
        out_specs=out_specs,
        out_shape=out_shape,
        compiler_params=_cparams(2),
        name=f"out_residual_{k_dim}",
    )(*args)
    return (out[0], out[1]) if nxt is not None else (out[0], None)


def _ffn_in_kernel(h_ref, wa_ref, wb_ref, o_ref, wab_ref, wbb_ref):
    @pl.when(pl.program_id(1) == 0)
    def _():
        wab_ref[...] = wa_ref[...].astype(BF16)
        wbb_ref[...] = wb_ref[...].astype(BF16)

    h = h_ref[...]
    a = jnp.dot(h, wab_ref[...], preferred_element_type=F32)
    b = jnp.dot(h, wbb_ref[...], preferred_element_type=F32)
    o_ref[...] = (a * jax.nn.sigmoid(a) * b).astype(BF16)


def _ffn_in(h, w_ffn_in, layer):
    tm, tn = 1024, 512
    n_n = FF_HIDDEN // tn
    return pl.pallas_call(
        _ffn_in_kernel,
        grid=(n_n, N_TOK // tm),
        in_specs=[
            pl.BlockSpec((tm, D_MODEL), lambda n, m: (m, 0)),
            pl.BlockSpec((None, D_MODEL, tn), lambda n, m: (layer, 0, n)),
            pl.BlockSpec((None, D_MODEL, tn), lambda n, m: (layer, 0, n_n + n)),
        ],
        out_specs=pl.BlockSpec((tm, tn), lambda n, m: (m, n)),
        out_shape=jax.ShapeDtypeStruct((N_TOK, FF_HIDDEN), BF16),
        scratch_shapes=[pltpu.VMEM((D_MODEL, tn), BF16), pltpu.VMEM((D_MODEL, tn), BF16)],
        compiler_params=_cparams(2),
        name="ffn_in",
    )(h, w_ffn_in, w_ffn_in)


MIX_SHIFT, MIX_SCALE, MIX_GATE, FFN_SHIFT, FFN_SCALE, FFN_GATE = range(6)


def kernel(x_prompt, x_sample, cache_na_k, cache_na_v, state_rglru, state_s5, c, c_ctx, w_mod, b_mod, g_mix_pre, g_mix_post, g_ffn_pre, g_ffn_post, w_in, rg_conv_w, rg_conv_b, rg_gate_w, rg_gate_b, rg_lambda, s5_lambda_re, s5_lambda_im, s5_log_dt, s5_b_re, s5_b_im, s5_c_re, s5_c_im, s5_d, s5_w_glu, na_rpb, w_rg_out, w_s5_out, w_na_out, w_o, w_ffn_in, w_ffn_out):
    x = jnp.concatenate([x_prompt.reshape(N_CTX_TOK, D_MODEL), x_sample.reshape(N_LAT_TOK, D_MODEL)], axis=0)
    cond = jnp.concatenate([c_ctx[None, :], c, jnp.zeros((N_COND - 1 - DEC_BATCH, D_MODEL), F32)], axis=0)
    mods = _modulation(cond, w_mod, b_mod)
    modv = mods.reshape(DEPTH, N_COND, 6, D_MODEL).transpose(0, 2, 1, 3).reshape(DEPTH * 6 * N_COND, 1, D_MODEL)

    row3 = lambda g: g.reshape(DEPTH, 1, D_MODEL)
    g_mix_pre, g_mix_post, g_ffn_pre, g_ffn_post = map(row3, (g_mix_pre, g_mix_post, g_ffn_pre, g_ffn_post))
    gate_w = rg_gate_w.transpose(0, 3, 4, 1, 2, 5).reshape(DEPTH, RG_HEADS, RG_BLOCK, 4 * RG_BLOCK).astype(BF16)
    gate_b = rg_gate_b.reshape(DEPTH, 4, RG_WIDTH)
    cache_k = cache_na_k.reshape(DEC_BATCH, DEPTH, PAST_LEN, NA_WIDTH)
    cache_v = cache_na_v.reshape(DEC_BATCH, DEPTH, PAST_LEN, NA_WIDTH)
    na_bias = _na_bias_tables(na_rpb)
    s5_h0 = state_s5.reshape(DEC_BATCH, DEPTH, 2, N_S5_CB, S5_CB_STATE, 2).transpose(1, 0, 3, 2, 5, 4)
    s5_h0 = s5_h0.reshape(DEPTH, DEC_BATCH, N_S5_CB, 4, S5_CB_STATE)
    rg_zero = jnp.zeros((BATCH, 2, RG_WIDTH), F32)
    s5_zero = jnp.zeros((BATCH, N_S5_CB, 4, S5_CB_STATE), F32)

    h = _prenorm(x, g_mix_pre, modv, 0, MIX_SHIFT, MIX_SCALE)
    ks, vs, rgs, s5s = [], [], [], []
    for l in range(DEPTH):
        proj = _in_proj(h, w_in, l)
        ks.append(proj[:N_CTX_TOK, OFF_K:OFF_V])
        vs.append(proj[:N_CTX_TOK, OFF_V:OFF_GATE])

        rg_args = (rg_conv_w[l], rg_conv_b[l][None, :], gate_w[l], gate_b[l], rg_lambda[l])
        y_rg_p, st_rg = _rglru(proj, *rg_args, rg_zero, length=SEQ, n_seq=BATCH, row0=0, heads=8)
        y_rg_s, _ = _rglru(proj, *rg_args, state_rglru[:, l], length=DEC_SEQ, n_seq=DEC_BATCH, row0=N_CTX_TOK, heads=4)
        rgs.append(st_rg)

        bmat, lamrow, cmat = _s5_tables(s5_lambda_re[l], s5_lambda_im[l], s5_log_dt[l], s5_b_re[l], s5_b_im[l],
                                        s5_c_re[l], s5_c_im[l])
        s5_args = (bmat, lamrow, cmat, s5_d[l][None, :])
        z_p, st_s5 = _s5(proj, *s5_args, s5_zero, length=SEQ, n_seq=BATCH, row0=0)
        z_s, _ = _s5(proj, *s5_args, s5_h0[l], length=DEC_SEQ, n_seq=DEC_BATCH, row0=N_CTX_TOK)
        s5s.append(st_s5)
        y_s5 = _s5_glu(jnp.concatenate([z_p, z_s], axis=0), s5_w_glu, l)

        y_na = jnp.concatenate([_ctx_attention(proj), _na_attention(proj, cache_k, cache_v, na_bias, l)], axis=0)
        y_rg = jnp.concatenate([y_rg_p, y_rg_s], axis=0)

        m = _merge(y_rg, y_s5, y_na, w_rg_out, w_s5_out, w_na_out, proj, l)
        x, h = _out_residual(m, w_o, x, g_mix_post, modv, l, MIX_GATE, (g_ffn_pre, l, FFN_SHIFT, FFN_SCALE))
        f = _ffn_in(h, w_ffn_in, l)
        nxt = (g_mix_pre, l + 1, MIX_SHIFT, MIX_SCALE) if l + 1 < DEPTH else None
        x, h = _out_residual(f, w_ffn_out, x, g_ffn_post, modv, l, FFN_GATE, nxt)

    y_prompt = x[:N_CTX_TOK].reshape(BATCH, SEQ, D_MODEL)
    y_sample = x[N_CTX_TOK:].reshape(DEC_BATCH, DEC_SEQ, D_MODEL)
    kv_shape = (BATCH, DEPTH, SEQ, NA_HEADS, NA_HEAD_DIM)
    new_k = jnp.stack([k.reshape(BATCH, SEQ, NA_WIDTH) for k in ks], axis=1).reshape(kv_shape)
    new_v = jnp.stack([v.reshape(BATCH, SEQ, NA_WIDTH) for v in vs], axis=1).reshape(kv_shape)
    new_rg = jnp.stack(rgs, axis=1)
    st = jnp.stack(s5s, axis=0).reshape(DEPTH, BATCH, N_S5_CB, 2, 2, S5_CB_GROUPS, S5_STATE)
    new_s5 = st.transpose(1, 0, 3, 2, 5, 6, 4).reshape(BATCH, DEPTH, 2, S5_GROUPS, S5_STATE, 2)
    return (y_prompt, y_sample, new_k, new_v, new_rg, new_s5)
```

```python
import functools
import math

import numpy as np
import jax
import jax.numpy as jnp
from jax import lax
from jax.experimental import pallas as pl
from jax.experimental.pallas import tpu as pltpu

F32 = jnp.float32
BF16 = jnp.bfloat16

D_MODEL = 2048
BATCH = 16
SEQ = 256
DEPTH = 4
DEC_BATCH = 4
DEC_SEQ = 1024
PAST_LEN = 512
GRID_W = 64
RG_WIDTH = 1024
RG_HEADS = 8
RG_BLOCK = RG_WIDTH // RG_HEADS
RG_CONV = 4
RG_C = 8.0
S5_WIDTH = 1024
S5_GROUP_CH = 16
S5_GROUPS = S5_WIDTH // S5_GROUP_CH
S5_STATE = 64
NA_HEADS = 8
NA_HEAD_DIM = 128
NA_WIDTH = NA_HEADS * NA_HEAD_DIM
NA_KH = 8
NA_KW = 16
NA_SCALE = NA_HEAD_DIM ** -0.5
NEG_INF = -1e30
N_BRANCH = 3
FF_HIDDEN = (8 * D_MODEL + 3 * 256 - 1) // (3 * 256) * 256
RMS_EPS = 1e-6
OFF_RG_X = 0
OFF_RG_G = OFF_RG_X + RG_WIDTH
OFF_S5 = OFF_RG_G + RG_WIDTH
OFF_Q = OFF_S5 + S5_WIDTH
OFF_K = OFF_Q + NA_WIDTH
OFF_V = OFF_K + NA_WIDTH
OFF_GATE = OFF_V + NA_WIDTH
IN_WIDTH = OFF_GATE + N_BRANCH * D_MODEL

N_CTX_TOK = BATCH * SEQ
N_LAT_TOK = DEC_BATCH * DEC_SEQ
N_TOK = N_CTX_TOK + N_LAT_TOK
N_COND = 8
GRID_ROWS = DEC_SEQ // GRID_W
S5_CB = 128
S5_CB_GROUPS = S5_CB // S5_GROUP_CH
S5_CB_STATE = S5_CB_GROUPS * S5_STATE
N_S5_CB = S5_WIDTH // S5_CB
SUBLANES = 8
S5_T = 4
S5_KC = S5_T * S5_CB
S5_ROWS = 256

VMEM_LIMIT = 52 * 1024 * 1024


def _cparams(n_grid):
    return pltpu.CompilerParams(dimension_semantics=("arbitrary",) * n_grid, vmem_limit_bytes=VMEM_LIMIT)


def _cond_row(start_row):
    return jnp.where(start_row < N_CTX_TOK, 0, 1 + (start_row - N_CTX_TOK) // DEC_SEQ)


def _mod_index(layer, k, start_row):
    return (layer * 6 + k) * N_COND + _cond_row(start_row)


def _rms(x):
    return x * lax.rsqrt(jnp.mean(x * x, axis=-1, keepdims=True) + RMS_EPS)


def _gelu_tanh(x):
    return 0.5 * x * (1.0 + jnp.tanh(math.sqrt(2.0 / math.pi) * (x + 0.044715 * (x * x * x))))


def _mod_kernel(c_ref, w_ref, b_ref, o_ref):
    cond = c_ref[...]
    a = (cond * jax.nn.sigmoid(cond)).astype(BF16)
    o_ref[...] = jnp.dot(a, w_ref[...].astype(BF16), preferred_element_type=F32) + b_ref[...]


def _modulation(cond, w_mod, b_mod):
    tn = 1024
    n_out = 6 * D_MODEL
    return pl.pallas_call(
        _mod_kernel,
        grid=(DEPTH, n_out // tn),
        in_specs=[
            pl.BlockSpec((N_COND, D_MODEL), lambda l, n: (0, 0)),
            pl.BlockSpec((None, D_MODEL, tn), lambda l, n: (l, 0, n)),
            pl.BlockSpec((None, 1, tn), lambda l, n: (l, 0, n)),
        ],
        out_specs=pl.BlockSpec((None, N_COND, tn), lambda l, n: (l, 0, n)),
        out_shape=jax.ShapeDtypeStruct((DEPTH, N_COND, n_out), F32),
        compiler_params=_cparams(2),
        name="modulation",
    )(cond, w_mod, b_mod.reshape(DEPTH, 1, n_out))


def _prenorm_kernel(x_ref, g_ref, sc_ref, sh_ref, o_ref):
    y = _rms(x_ref[...]) * g_ref[...]
    o_ref[...] = (y * (1.0 + sc_ref[...]) + sh_ref[...]).astype(BF16)


def _prenorm(x, g_all, modv, layer, k_shift, k_scale):
    tm = 512
    return pl.pallas_call(
        _prenorm_kernel,
        grid=(N_TOK // tm,),
        in_specs=[
            pl.BlockSpec((tm, D_MODEL), lambda i: (i, 0)),
            pl.BlockSpec((None, 1, D_MODEL), lambda i: (layer, 0, 0)),
            pl.BlockSpec((None, 1, D_MODEL), lambda i: (_mod_index(layer, k_scale, i * tm), 0, 0)),
            pl.BlockSpec((None, 1, D_MODEL), lambda i: (_mod_index(layer, k_shift, i * tm), 0, 0)),
        ],
        out_specs=pl.BlockSpec((tm, D_MODEL), lambda i: (i, 0)),
        out_shape=jax.ShapeDtypeStruct((N_TOK, D_MODEL), BF16),
        compiler_params=_cparams(1),
        name="prenorm",
    )(x, g_all, modv, modv)


def _proj_kernel(a_ref, w_ref, o_ref, wb_ref):
    @pl.when(pl.program_id(1) == 0)
    def _():
        wb_ref[...] = w_ref[...].astype(BF16)

    o_ref[...] = jnp.dot(a_ref[...], wb_ref[...], preferred_element_type=F32)


def _matmul(a, w, layer, *, tm, tn, name):
    k_dim, n_dim = w.shape[1], w.shape[2]
    return pl.pallas_call(
        _proj_kernel,
        grid=(n_dim // tn, N_TOK // tm),
        in_specs=[
            pl.BlockSpec((tm, k_dim), lambda n, m: (m, 0)),
            pl.BlockSpec((None, k_dim, tn), lambda n, m: (layer, 0, n)),
        ],
        out_specs=pl.BlockSpec((tm, tn), lambda n, m: (m, n)),
        out_shape=jax.ShapeDtypeStruct((N_TOK, n_dim), F32),
        scratch_shapes=[pltpu.VMEM((k_dim, tn), BF16)],
        compiler_params=_cparams(2),
        name=name,
    )(a, w)


def _tile_scan(a, b, carry, reverse):
    row = lax.broadcasted_iota(jnp.int32, a.shape, 0)
    for d in (1, 2, 4):
        if reverse:
            keep = row < SUBLANES - d
            shift = SUBLANES - d
        else:
            keep = row >= d
            shift = d
        sa = jnp.where(keep, pltpu.roll(a, shift, 0), 1.0)
        sb = jnp.where(keep, pltpu.roll(b, shift, 0), 0.0)
        b = a * sb + b
        a = a * sa
    h = a * carry + b
    return h, (h[0:1] if reverse else h[SUBLANES - 1:SUBLANES])


RG_ROWS = 128


def _rg_kernel(xr_ref, xg_ref, cw_ref, cb_ref, gw_ref, gb_ref, lam_ref, h0_ref, y_ref, st_ref,
               xp_ref, a_ref, b_ref, hf_ref, *, length, heads):
    pad = SUBLANES
    win = RG_ROWS + 2 * pad
    zeros = jnp.zeros((pad, heads * RG_BLOCK), F32)
    xp_ref[0:pad, :] = zeros
    xp_ref[pad + length:pad + length + pad, :] = zeros
    xp_ref[pad:pad + length, :] = xr_ref[...]
    neg = -lam_ref[...]
    softplus = jnp.maximum(neg, 0.0) + jnp.log1p(jnp.exp(-jnp.abs(neg)))

    def gates(c, _):
        r0 = pl.multiple_of(c * RG_ROWS, RG_ROWS)
        w = xp_ref[pl.ds(r0, win), :]
        tap = lambda s: pltpu.roll(w, s % win, 0)[pad:pad + RG_ROWS]
        x = (cw_ref[1:2, :] * w[pad:pad + RG_ROWS] + cb_ref[...] + cw_ref[0:1, :] * tap(1)
             + cw_ref[2:3, :] * tap(-1) + cw_ref[3:4, :] * tap(-2))
        xb = x.astype(BF16)
        for j in range(heads):
            cs = slice(j * RG_BLOCK, (j + 1) * RG_BLOCK)
            gl = jnp.dot(xb[:, cs], gw_ref[j], preferred_element_type=F32)
            for d in range(2):
                r = jax.nn.sigmoid(gl[:, (2 * d) * RG_BLOCK:(2 * d + 1) * RG_BLOCK] + gb_ref[2 * d:2 * d + 1, cs])
                i = jax.nn.sigmoid(gl[:, (2 * d + 1) * RG_BLOCK:(2 * d + 2) * RG_BLOCK]
                                   + gb_ref[2 * d + 1:2 * d + 2, cs])
                log_a = (-RG_C) * r * softplus[d:d + 1, cs]
                t = jnp.tanh(log_a)
                a_ref[d, pl.ds(r0, RG_ROWS), cs] = jnp.exp(log_a)
                b_ref[d, pl.ds(r0, RG_ROWS), cs] = jnp.sqrt(-2.0 * t / (1.0 - t)) * i * x[:, cs]
        return 0

    lax.fori_loop(0, length // RG_ROWS, gates, 0)
    n_tiles = length // SUBLANES

    def fwd(k, carry):
        r0 = pl.multiple_of(k * SUBLANES, SUBLANES)
        h, carry = _tile_scan(a_ref[0, pl.ds(r0, SUBLANES), :], b_ref[0, pl.ds(r0, SUBLANES), :], carry, False)
        hf_ref[pl.ds(r0, SUBLANES), :] = h
        return carry

    def bwd(k, carry):
        r0 = pl.multiple_of((n_tiles - 1 - k) * SUBLANES, SUBLANES)
        h, carry = _tile_scan(a_ref[1, pl.ds(r0, SUBLANES), :], b_ref[1, pl.ds(r0, SUBLANES), :], carry, True)
        y = (hf_ref[pl.ds(r0, SUBLANES), :] + h) * _gelu_tanh(xg_ref[pl.ds(r0, SUBLANES), :])
        y_ref[pl.ds(r0, SUBLANES), :] = y.astype(BF16)
        return carry

    st_ref[0:1, :] = lax.fori_loop(0, n_tiles, fwd, h0_ref[0:1, :])
    st_ref[1:2, :] = lax.fori_loop(0, n_tiles, bwd, h0_ref[1:2, :])


def _alias_kernel(body, n_in):
    def wrapped(*refs, **kw):
        return body(*refs[:n_in], *refs[n_in + 1:], **kw)
    return wrapped


def _rglru(proj, cw, cb, gw, gb, lam, h0, y_prev, *, length, n_seq, row0, heads):
    width = heads * RG_BLOCK
    n_cb = RG_WIDTH // width
    rb0 = row0 // length
    body = functools.partial(_rg_kernel, length=length, heads=heads)
    extra_specs, extra_args, aliases = [], [], {}
    if y_prev is not None:
        body = functools.partial(_alias_kernel(_rg_kernel, 8), length=length, heads=heads)
        extra_specs, extra_args, aliases = [pl.BlockSpec(memory_space=pl.ANY)], [y_prev], {8: 0}
    y, st = pl.pallas_call(
        body,
        grid=(n_seq, n_cb),
        input_output_aliases=aliases,
        in_specs=[
            pl.BlockSpec((length, width), lambda s, c: (rb0 + s, OFF_RG_X // width + c)),
            pl.BlockSpec((length, width), lambda s, c: (rb0 + s, OFF_RG_G // width + c)),
            pl.BlockSpec((RG_CONV, width), lambda s, c: (0, c)),
            pl.BlockSpec((1, width), lambda s, c: (0, c)),
            pl.BlockSpec((heads, RG_BLOCK, 4 * RG_BLOCK), lambda s, c: (c, 0, 0)),
            pl.BlockSpec((4, width), lambda s, c: (0, c)),
            pl.BlockSpec((2, width), lambda s, c: (0, c)),
            pl.BlockSpec((None, 2, width), lambda s, c: (s, 0, c)),
        ] + extra_specs,
        out_specs=[
            pl.BlockSpec((length, width), lambda s, c: (rb0 + s, c)),
            pl.BlockSpec((None, 2, width), lambda s, c: (s, 0, c)),
        ],
        out_shape=[
            jax.ShapeDtypeStruct((N_TOK, RG_WIDTH), BF16),
            jax.ShapeDtypeStruct((n_seq, 2, RG_WIDTH), F32),
        ],
        scratch_shapes=[
            pltpu.VMEM((length + 2 * SUBLANES, width), F32),
            pltpu.VMEM((2, length, width), F32),
            pltpu.VMEM((2, length, width), F32),
            pltpu.VMEM((length, width), F32),
        ],
        compiler_params=_cparams(2),
        name=f"rglru_{length}",
    )(proj, proj, cw, cb, gw, gb, lam, h0, *extra_args)
    return y, st


def _cmul(ar, ai, br, bi):
    return ar * br - ai * bi, ar * bi + ai * br


def _s5_kernel(u_ref, m_ref, g_ref, f_ref, lam_ref, d_ref, h0_ref, z_ref, st_ref, uc_ref, y_ref, e_ref,
               *, n_sub, sub_len, halves):
    ns = S5_CB_STATE
    n_c = sub_len // S5_T
    rows = n_sub * n_c
    for c in range(n_c):
        for t in range(S5_T):
            uc_ref[c * n_sub:(c + 1) * n_sub, t * S5_CB:(t + 1) * S5_CB] = (
                u_ref[pl.ds(c * S5_T + t, n_sub, stride=sub_len), :])

    def proj_in(i, _):
        r0 = pl.multiple_of(i * S5_ROWS, S5_ROWS)
        ub = uc_ref[pl.ds(r0, S5_ROWS), :].astype(BF16)
        y_ref[pl.ds(r0, S5_ROWS), :] = jnp.dot(ub, m_ref[...], preferred_element_type=F32)
        e_ref[pl.ds(r0, S5_ROWS), :] = jnp.dot(ub, g_ref[...], preferred_element_type=F32)
        return 0

    lax.fori_loop(0, rows // S5_ROWS, proj_in, 0)

    def run(d, init, store):
        cr = slice((2 * d) * ns, (2 * d + 1) * ns)
        ci = slice((2 * d + 1) * ns, (2 * d + 2) * ns)

        def body(k, s):
            c = k if d == 0 else n_c - 1 - k
            r0 = pl.multiple_of(c * n_sub, n_sub)
            er = e_ref[pl.ds(r0, n_sub), cr]
            ei = e_ref[pl.ds(r0, n_sub), ci]
            if store:
                e_ref[pl.ds(r0, n_sub), cr] = s[0]
                e_ref[pl.ds(r0, n_sub), ci] = s[1]
            nr, ni = _cmul(lam_ref[2 * d:2 * d + 1, :], lam_ref[2 * d + 1:2 * d + 2, :], s[0], s[1])
            return nr + er, ni + ei

        return lax.fori_loop(0, n_c, body, init)

    zero = jnp.zeros((n_sub, ns), F32)
    second = (lax.broadcasted_iota(jnp.int32, (n_sub, ns), 0) & 1) == 1
    for d in range(2):
        cr = slice((2 * d) * ns, (2 * d + 1) * ns)
        ci = slice((2 * d + 1) * ns, (2 * d + 2) * ns)
        init = (h0_ref[:, cr], h0_ref[:, ci])
        if halves == 2:
            er, ei = run(d, (zero, zero), False)
            pr, pi = lam_ref[2 * d:2 * d + 1, :], lam_ref[2 * d + 1:2 * d + 2, :]
            for _ in range(int(math.log2(n_c))):
                pr, pi = _cmul(pr, pi, pr, pi)
            hr, hi = _cmul(pr, pi, *init)
            shift = 1 if d == 0 else n_sub - 1
            hr, hi = hr + pltpu.roll(er, shift, 0), hi + pltpu.roll(ei, shift, 0)
            runs_first = jnp.logical_not(second) if d == 0 else second
            init = (jnp.where(runs_first, init[0], hr), jnp.where(runs_first, init[1], hi))
        sr, si = run(d, init, True)
        st_ref[:, cr] = sr
        st_ref[:, ci] = si

    def proj_out(i, _):
        r0 = pl.multiple_of(i * S5_ROWS, S5_ROWS)
        y = (y_ref[pl.ds(r0, S5_ROWS), :]
             + jnp.dot(e_ref[pl.ds(r0, S5_ROWS), :].astype(BF16), f_ref[...], preferred_element_type=F32)
             + d_ref[...] * uc_ref[pl.ds(r0, S5_ROWS), :])
        y_ref[pl.ds(r0, S5_ROWS), :] = _gelu_tanh(y)
        return 0

    lax.fori_loop(0, rows // S5_ROWS, proj_out, 0)
    for c in range(n_c):
        for t in range(S5_T):
            z_ref[pl.ds(c * S5_T + t, n_sub, stride=sub_len), :] = (
                y_ref[c * n_sub:(c + 1) * n_sub, t * S5_CB:(t + 1) * S5_CB])


def _s5(proj, tables, dvec, h0, z_prev, *, n_sub, sub_len, halves, group):
    mmat, gmat, fmat, lam_t = tables
    ns = S5_CB_STATE
    n_rows = n_sub * sub_len
    body = functools.partial(_s5_kernel, n_sub=n_sub, sub_len=sub_len, halves=halves)
    extra_specs, extra_args, aliases = [], [], {}
    if z_prev is not None:
        body = functools.partial(_alias_kernel(_s5_kernel, 7), n_sub=n_sub, sub_len=sub_len, halves=halves)
        extra_specs, extra_args, aliases = [pl.BlockSpec(memory_space=pl.ANY)], [z_prev], {7: 0}
    return pl.pallas_call(
        body,
        grid=(N_S5_CB,),
        input_output_aliases=aliases,
        in_specs=[
            pl.BlockSpec((n_rows, S5_CB), lambda c: (group, OFF_S5 // S5_CB + c)),
            pl.BlockSpec((None, S5_KC, S5_KC), lambda c: (c, 0, 0)),
            pl.BlockSpec((None, S5_KC, 4 * ns), lambda c: (c, 0, 0)),
            pl.BlockSpec((None, 4 * ns, S5_KC), lambda c: (c, 0, 0)),
            pl.BlockSpec((None, 4, ns), lambda c: (c, 0, 0)),
            pl.BlockSpec((None, 1, S5_KC), lambda c: (c, 0, 0)),
            pl.BlockSpec((None, n_sub, 4 * ns), lambda c: (c, 0, 0)),
        ] + extra_specs,
        out_specs=[
            pl.BlockSpec((n_rows, S5_CB), lambda c: (group, c)),
            pl.BlockSpec((None, n_sub, 4 * ns), lambda c: (c, 0, 0)),
        ],
        out_shape=[
            jax.ShapeDtypeStruct((N_TOK, S5_WIDTH), F32),
            jax.ShapeDtypeStruct((N_S5_CB, n_sub, 4 * ns), F32),
        ],
        scratch_shapes=[
            pltpu.VMEM((n_rows // S5_T, S5_KC), F32),
            pltpu.VMEM((n_rows // S5_T, S5_KC), F32),
            pltpu.VMEM((n_rows // S5_T, 4 * ns), F32),
        ],
        compiler_params=_cparams(1),
        name=f"s5_{sub_len}",
    )(proj, mmat, gmat, fmat, lam_t, dvec, h0, *extra_args)


def _s5_tables(lam_re, lam_im, log_dt, b_re, b_im, c_re, c_im):
    hp = lax.Precision.HIGHEST
    n_t = S5_T
    dt = jnp.exp(log_dt)[..., None]
    mag = jnp.exp(lam_re * dt)
    lr, li = mag * jnp.cos(lam_im * dt), mag * jnp.sin(lam_im * dt)
    den = lam_re * lam_re + lam_im * lam_im
    qr = ((lr - 1.0) * lam_re + li * lam_im) / den
    qi = (li * lam_re - (lr - 1.0) * lam_im) / den
    bbr = qr[..., None] * b_re - qi[..., None] * b_im
    bbi = qr[..., None] * b_im + qi[..., None] * b_re
    powers = [(jnp.ones_like(lr), jnp.zeros_like(lr))]
    for _ in range(n_t):
        powers.append(_cmul(*powers[-1], lr, li))
    pr = jnp.stack([p[0] for p in powers])
    pi = jnp.stack([p[1] for p in powers])
    lbr = pr[..., None] * bbr - pi[..., None] * bbi
    lbi = pr[..., None] * bbi + pi[..., None] * bbr
    wr = c_re * pr[:, :, :, None, :] - c_im * pi[:, :, :, None, :]
    wi = c_re * pi[:, :, :, None, :] + c_im * pr[:, :, :, None, :]
    kern = (jnp.einsum('kdgop,dgpi->kdgoi', wr, bbr, precision=hp)
            - jnp.einsum('kdgop,dgpi->kdgoi', wi, bbi, precision=hp))

    eye = jnp.eye(S5_CB_GROUPS, dtype=F32)
    blocks = (N_S5_CB, S5_CB_GROUPS)
    s_idx, t_idx = np.arange(n_t)[:, None], np.arange(n_t)[None, :]
    causal = (t_idx >= s_idx)[:, :, None, None, None]
    anti = (s_idx >= t_idx)[:, :, None, None, None]
    mst = (jnp.where(causal, kern[np.clip(t_idx - s_idx, 0, n_t), 0], 0.0)
           + jnp.where(anti, kern[np.clip(s_idx - t_idx, 0, n_t), 1], 0.0))
    mst = mst.reshape(n_t, n_t, *blocks, S5_GROUP_CH, S5_GROUP_CH)
    mmat = jnp.einsum('stkgoi,gh->ksgitho', mst, eye).reshape(N_S5_CB, S5_KC, S5_KC)

    p_f, p_b = np.arange(n_t)[::-1], np.arange(n_t)
    gst = jnp.stack([jnp.stack([lbr[p_f, 0], lbi[p_f, 0]]), jnp.stack([lbr[p_b, 1], lbi[p_b, 1]])])
    gst = gst.reshape(2, 2, n_t, *blocks, S5_STATE, S5_GROUP_CH)
    gmat = jnp.einsum('dcskgpi,gh->ksgidchp', gst, eye).reshape(N_S5_CB, S5_KC, 4 * S5_CB_STATE)

    q_f, q_b = np.arange(n_t) + 1, n_t - np.arange(n_t)
    fst = jnp.stack([jnp.stack([wr[q_f, 0], -wi[q_f, 0]]), jnp.stack([wr[q_b, 1], -wi[q_b, 1]])])
    fst = fst.reshape(2, 2, n_t, *blocks, S5_GROUP_CH, S5_STATE)
    fmat = jnp.einsum('dctkgop,gh->kdchptgo', fst, eye).reshape(N_S5_CB, 4 * S5_CB_STATE, S5_KC)

    lam_t = jnp.stack([pr[n_t], pi[n_t]], axis=1)
    lam_t = lam_t.reshape(2, 2, N_S5_CB, S5_CB_STATE).transpose(2, 0, 1, 3).reshape(N_S5_CB, 4, S5_CB_STATE)
    return mmat.astype(BF16), gmat.astype(BF16), fmat.astype(BF16), lam_t


def _s5_glu_kernel(z_ref, w_ref, o_ref):
    z = z_ref[...]
    g = jnp.dot(z.astype(BF16), w_ref[...].astype(BF16), preferred_element_type=F32)
    o_ref[...] = (z * jax.nn.sigmoid(g)).astype(BF16)


def _s5_glu(z, w_glu, layer):
    tm = 1024
    return pl.pallas_call(
        _s5_glu_kernel,
        grid=(N_TOK // tm,),
        in_specs=[
            pl.BlockSpec((tm, S5_WIDTH), lambda i: (i, 0)),
            pl.BlockSpec((None, S5_WIDTH, S5_WIDTH), lambda i: (layer, 0, 0)),
        ],
        out_specs=pl.BlockSpec((tm, S5_WIDTH), lambda i: (i, 0)),
        out_shape=jax.ShapeDtypeStruct((N_TOK, S5_WIDTH), BF16),
        compiler_params=_cparams(1),
        name="s5_glu",
    )(z, w_glu)


def _dot_nt(a, b):
    return lax.dot_general(a, b, (((1,), (1,)), ((), ())), preferred_element_type=F32)


def _ctx_attn_kernel(q_ref, k_ref, v_ref, o_ref):
    for h in range(NA_HEADS):
        cs = slice(h * NA_HEAD_DIM, (h + 1) * NA_HEAD_DIM)
        q = (q_ref[:, cs] * NA_SCALE).astype(BF16)
        s = _dot_nt(q, k_ref[:, cs].astype(BF16))
        p = jnp.exp(s - jnp.max(s, axis=-1, keepdims=True))
        p = (p / jnp.sum(p, axis=-1, keepdims=True)).astype(BF16)
        o_ref[:, cs] = jnp.dot(p, v_ref[:, cs].astype(BF16), preferred_element_type=F32).astype(BF16)


def _ctx_attention(proj):
    spec = lambda off: pl.BlockSpec((SEQ, NA_WIDTH), lambda b: (b, off // NA_WIDTH))
    return pl.pallas_call(
        _ctx_attn_kernel,
        grid=(BATCH,),
        in_specs=[spec(OFF_Q), spec(OFF_K), spec(OFF_V)],
        out_specs=pl.BlockSpec((SEQ, NA_WIDTH), lambda b: (b, 0)),
        out_shape=jax.ShapeDtypeStruct((N_TOK, NA_WIDTH), BF16),
        compiler_params=_cparams(1),
        name="ctx_attention",
    )(proj, proj, proj)


def _na_row_start(r):
    return min(max(r - NA_KH // 2, 0), GRID_ROWS - NA_KH)


def _na_variant(r):
    lo, hi = NA_KH // 2, GRID_ROWS - NA_KH // 2
    if r < lo:
        return r
    if r <= hi:
        return lo
    return r - hi + lo


N_NA_VARIANTS = _na_variant(GRID_ROWS - 1) + 1
NA_LOCAL = NA_KH * GRID_W


def _na_kernel(q_ref, k_ref, v_ref, kc_ref, vc_ref, bias_ref, o_ref):
    k = k_ref[...].astype(BF16)
    v = v_ref[...].astype(BF16)
    kc = kc_ref[...].astype(BF16)
    vc = vc_ref[...].astype(BF16)
    for r in range(GRID_ROWS):
        k0 = _na_row_start(r) * GRID_W
        q = (q_ref[r * GRID_W:(r + 1) * GRID_W, :] * NA_SCALE).astype(BF16)
        s_loc = _dot_nt(q, k[k0:k0 + NA_LOCAL]) + bias_ref[_na_variant(r)]
        s_ctx = _dot_nt(q, kc)
        m = jnp.maximum(jnp.max(s_loc, axis=-1, keepdims=True), jnp.max(s_ctx, axis=-1, keepdims=True))
        p_loc = jnp.exp(s_loc - m)
        p_ctx = jnp.exp(s_ctx - m)
        inv = 1.0 / (jnp.sum(p_loc, axis=-1, keepdims=True) + jnp.sum(p_ctx, axis=-1, keepdims=True))
        o = (jnp.dot((p_loc * inv).astype(BF16), v[k0:k0 + NA_LOCAL], preferred_element_type=F32)
             + jnp.dot((p_ctx * inv).astype(BF16), vc, preferred_element_type=F32))
        o_ref[r * GRID_W:(r + 1) * GRID_W, :] = o.astype(BF16)


def _na_attention(proj, cache_k, cache_v, bias, y_ctx, layer):
    rb0 = N_CTX_TOK // DEC_SEQ
    spec = lambda off: pl.BlockSpec((DEC_SEQ, NA_HEAD_DIM), lambda b, h: (rb0 + b, off // NA_HEAD_DIM + h))
    cspec = pl.BlockSpec((None, None, PAST_LEN, NA_HEAD_DIM), lambda b, h: (b, layer, 0, h))
    return pl.pallas_call(
        _alias_kernel(_na_kernel, 6),
        grid=(DEC_BATCH, NA_HEADS),
        input_output_aliases={6: 0},
        in_specs=[spec(OFF_Q), spec(OFF_K), spec(OFF_V), cspec, cspec,
                  pl.BlockSpec((None, None, N_NA_VARIANTS, GRID_W, NA_LOCAL), lambda b, h: (layer, h, 0, 0, 0)),
                  pl.BlockSpec(memory_space=pl.ANY)],
        out_specs=pl.BlockSpec((DEC_SEQ, NA_HEAD_DIM), lambda b, h: (rb0 + b, h)),
        out_shape=jax.ShapeDtypeStruct((N_TOK, NA_WIDTH), BF16),
        compiler_params=_cparams(2),
        name="na_attention",
    )(proj, proj, proj, cache_k, cache_v, bias, y_ctx)


def _na_bias_tables(na_rpb):
    qc = np.arange(GRID_W)[:, None]
    kc = np.arange(GRID_W)[None, :]
    win = np.clip(qc - NA_KW // 2, 0, GRID_W - NA_KW)
    in_win = (kc >= win) & (kc < win + NA_KW)
    col = np.clip(kc - qc, -(NA_KW - 1), NA_KW - 1) + NA_KW - 1
    toep = jnp.where(in_win, na_rpb[:, :, :, col], NEG_INF)
    rows = [next(r for r in range(GRID_ROWS) if _na_variant(r) == v) for v in range(N_NA_VARIANTS)]
    ridx = np.array([[_na_row_start(r) + i - r + NA_KH - 1 for i in range(NA_KH)] for r in rows])
    tab = toep[:, :, ridx]
    return tab.transpose(0, 1, 2, 4, 3, 5).reshape(DEPTH, NA_HEADS, N_NA_VARIANTS, GRID_W, NA_LOCAL)


def _merge_kernel(y0_ref, y1_ref, y2_ref, w0_ref, w1_ref, w2_ref, g0_ref, g1_ref, g2_ref, o_ref, wb_ref):
    @pl.when(pl.program_id(1) == 0)
    def _():
        wb_ref[0] = w0_ref[...].astype(BF16)
        wb_ref[1] = w1_ref[...].astype(BF16)
        wb_ref[2] = w2_ref[...].astype(BF16)

    m = jax.nn.sigmoid(g0_ref[...]) * jnp.dot(y0_ref[...], wb_ref[0], preferred_element_type=F32)
    m = m + jax.nn.sigmoid(g1_ref[...]) * jnp.dot(y1_ref[...], wb_ref[1], preferred_element_type=F32)
    m = m + jax.nn.sigmoid(g2_ref[...]) * jnp.dot(y2_ref[...], wb_ref[2], preferred_element_type=F32)
    o_ref[...] = m.astype(BF16)


def _merge(y_rg, y_s5, y_na, w_rg, w_s5, w_na, proj, layer):
    tm, tn = 256, 1024
    width = RG_WIDTH
    yspec = pl.BlockSpec((tm, width), lambda n, m: (m, 0))
    wspec = pl.BlockSpec((None, width, tn), lambda n, m: (layer, 0, n))
    gspec = lambda b: pl.BlockSpec((tm, tn), lambda n, m: (m, (OFF_GATE + b * D_MODEL) // tn + n))
    return pl.pallas_call(
        _merge_kernel,
        grid=(D_MODEL // tn, N_TOK // tm),
        in_specs=[yspec, yspec, yspec, wspec, wspec, wspec, gspec(0), gspec(1), gspec(2)],
        out_specs=pl.BlockSpec((tm, tn), lambda n, m: (m, n)),
        out_shape=jax.ShapeDtypeStruct((N_TOK, D_MODEL), BF16),
        scratch_shapes=[pltpu.VMEM((N_BRANCH, width, tn), BF16)],
        compiler_params=_cparams(2),
        name="merge",
    )(y_rg, y_s5, y_na, w_rg, w_s5, w_na, proj, proj, proj)


def _res_kernel(*refs, emit_next):
    if emit_next:
        y_ref, x_ref, gp_ref, gate_ref, gn_ref, sc_ref, sh_ref, xo_ref, ho_ref = refs
    else:
        y_ref, x_ref, gp_ref, gate_ref, xo_ref = refs
    xn = x_ref[...] + gate_ref[...] * (_rms(y_ref[...]) * gp_ref[...])
    xo_ref[...] = xn
    if emit_next:
        ho_ref[...] = ((_rms(xn) * gn_ref[...]) * (1.0 + sc_ref[...]) + sh_ref[...]).astype(BF16)


def _residual(y, x, g_post, modv, layer, k_gate, nxt):
    tm = 512
    row = lambda lyr: pl.BlockSpec((None, 1, D_MODEL), lambda i: (lyr, 0, 0))
    modrow = lambda lyr, kk: pl.BlockSpec((None, 1, D_MODEL), lambda i: (_mod_index(lyr, kk, i * tm), 0, 0))
    xspec = pl.BlockSpec((tm, D_MODEL), lambda i: (i, 0))
    in_specs = [xspec, xspec, row(layer), modrow(layer, k_gate)]
    args = [y, x, g_post, modv]
    out_specs = [xspec]
    out_shape = [jax.ShapeDtypeStruct((N_TOK, D_MODEL), F32)]
    if nxt is not None:
        g_next, l_next, k_shift, k_scale = nxt
        in_specs += [row(l_next), modrow(l_next, k_scale), modrow(l_next, k_shift)]
        args += [g_next, modv, modv]
        out_specs.append(xspec)
        out_shape.append(jax.ShapeDtypeStruct((N_TOK, D_MODEL), BF16))
    out = pl.pallas_call(
        functools.partial(_res_kernel, emit_next=nxt is not None),
        grid=(N_TOK // tm,),
        in_specs=in_specs,
        out_specs=out_specs,
        out_shape=out_shape,
        compiler_params=_cparams(1),
        name="residual",
    )(*args)
    return (out[0], out[1]) if nxt is not None else (out[0], None)


def _ffn_in_kernel(h_ref, wa_ref, wb_ref, o_ref, wab_ref, wbb_ref):
    @pl.when(pl.program_id(1) == 0)
    def _():
        wab_ref[...] = wa_ref[...].astype(BF16)
        wbb_ref[...] = wb_ref[...].astype(BF16)

    h = h_ref[...]
    a = jnp.dot(h, wab_ref[...], preferred_element_type=F32)
    b = jnp.dot(h, wbb_ref[...], preferred_element_type=F32)
    o_ref[...] = (a * jax.nn.sigmoid(a) * b).astype(BF16)


def _ffn_in(h, w_ffn_in, layer):
    tm, tn = 1024, 512
    n_n = FF_HIDDEN // tn
    return pl.pallas_call(
        _ffn_in_kernel,
        grid=(n_n, N_TOK // tm),
        in_specs=[
            pl.BlockSpec((tm, D_MODEL), lambda n, m: (m, 0)),
            pl.BlockSpec((None, D_MODEL, tn), lambda n, m: (layer, 0, n)),
            pl.BlockSpec((None, D_MODEL, tn), lambda n, m: (layer, 0, n_n + n)),
        ],
        out_specs=pl.BlockSpec((tm, tn), lambda n, m: (m, n)),
        out_shape=jax.ShapeDtypeStruct((N_TOK, FF_HIDDEN), BF16),
        scratch_shapes=[pltpu.VMEM((D_MODEL, tn), BF16), pltpu.VMEM((D_MODEL, tn), BF16)],
        compiler_params=_cparams(2),
        name="ffn_in",
    )(h, w_ffn_in, w_ffn_in)


MIX_SHIFT, MIX_SCALE, MIX_GATE, FFN_SHIFT, FFN_SCALE, FFN_GATE = range(6)


def kernel(x_prompt, x_sample, cache_na_k, cache_na_v, state_rglru, state_s5, c, c_ctx, w_mod, b_mod, g_mix_pre, g_mix_post, g_ffn_pre, g_ffn_post, w_in, rg_conv_w, rg_conv_b, rg_gate_w, rg_gate_b, rg_lambda, s5_lambda_re, s5_lambda_im, s5_log_dt, s5_b_re, s5_b_im, s5_c_re, s5_c_im, s5_d, s5_w_glu, na_rpb, w_rg_out, w_s5_out, w_na_out, w_o, w_ffn_in, w_ffn_out):
    x = jnp.concatenate([x_prompt.reshape(N_CTX_TOK, D_MODEL), x_sample.reshape(N_LAT_TOK, D_MODEL)], axis=0)
    cond = jnp.concatenate([c_ctx[None, :], c, jnp.zeros((N_COND - 1 - DEC_BATCH, D_MODEL), F32)], axis=0)
    mods = _modulation(cond, w_mod, b_mod)
    modv = mods.reshape(DEPTH, N_COND, 6, D_MODEL).transpose(0, 2, 1, 3).reshape(DEPTH * 6 * N_COND, 1, D_MODEL)

    row3 = lambda g: g.reshape(DEPTH, 1, D_MODEL)
    g_mix_pre, g_mix_post, g_ffn_pre, g_ffn_post = map(row3, (g_mix_pre, g_mix_post, g_ffn_pre, g_ffn_post))
    gate_w = rg_gate_w.transpose(0, 3, 4, 1, 2, 5).reshape(DEPTH, RG_HEADS, RG_BLOCK, 4 * RG_BLOCK).astype(BF16)
    gate_b = rg_gate_b.reshape(DEPTH, 4, RG_WIDTH)
    cache_k = cache_na_k.reshape(DEC_BATCH, DEPTH, PAST_LEN, NA_WIDTH)
    cache_v = cache_na_v.reshape(DEC_BATCH, DEPTH, PAST_LEN, NA_WIDTH)
    na_bias = _na_bias_tables(na_rpb)
    s5_h0 = state_s5.reshape(DEC_BATCH, DEPTH, 2, N_S5_CB, S5_CB_STATE, 2).transpose(1, 3, 0, 2, 5, 4)
    s5_h0 = jnp.repeat(s5_h0.reshape(DEPTH, N_S5_CB, DEC_BATCH, 4 * S5_CB_STATE), 2, axis=2)
    rg_zero = jnp.zeros((BATCH, 2, RG_WIDTH), F32)
    s5_zero = jnp.zeros((N_S5_CB, BATCH, 4 * S5_CB_STATE), F32)
    s5_skip = jnp.tile(s5_d.reshape(DEPTH, N_S5_CB, 1, S5_CB), (1, 1, 1, S5_T))

    h = _prenorm(x, g_mix_pre, modv, 0, MIX_SHIFT, MIX_SCALE)
    ks, vs, rgs, s5s = [], [], [], []
    for l in range(DEPTH):
        proj = _matmul(h, w_in, l, tm=1024, tn=1024, name="in_proj")
        ks.append(proj[:N_CTX_TOK, OFF_K:OFF_V])
        vs.append(proj[:N_CTX_TOK, OFF_V:OFF_GATE])

        rg_args = (rg_conv_w[l], rg_conv_b[l][None, :], gate_w[l], gate_b[l], rg_lambda[l])
        y_rg, st_rg = _rglru(proj, *rg_args, rg_zero, None, length=SEQ, n_seq=BATCH, row0=0, heads=8)
        y_rg, _ = _rglru(proj, *rg_args, state_rglru[:, l], y_rg, length=DEC_SEQ, n_seq=DEC_BATCH, row0=N_CTX_TOK, heads=4)
        rgs.append(st_rg)

        tables = _s5_tables(s5_lambda_re[l], s5_lambda_im[l], s5_log_dt[l], s5_b_re[l], s5_b_im[l],
                            s5_c_re[l], s5_c_im[l])
        z, st_s5 = _s5(proj, tables, s5_skip[l], s5_zero, None, n_sub=BATCH, sub_len=SEQ, halves=1, group=0)
        z, _ = _s5(proj, tables, s5_skip[l], s5_h0[l], z, n_sub=2 * DEC_BATCH, sub_len=DEC_SEQ // 2, halves=2, group=1)
        s5s.append(st_s5)
        y_s5 = _s5_glu(z, s5_w_glu, l)

        y_na = _na_attention(proj, cache_k, cache_v, na_bias, _ctx_attention(proj), l)

        m = _merge(y_rg, y_s5, y_na, w_rg_out, w_s5_out, w_na_out, proj, l)
        y = _matmul(m, w_o, l, tm=1024, tn=1024, name="out_proj")
        x, h = _residual(y, x, g_mix_post, modv, l, MIX_GATE, (g_ffn_pre, l, FFN_SHIFT, FFN_SCALE))
        f = _ffn_in(h, w_ffn_in, l)
        y = _matmul(f, w_ffn_out, l, tm=512, tn=512, name="ffn_out")
        nxt = (g_mix_pre, l + 1, MIX_SHIFT, MIX_SCALE) if l + 1 < DEPTH else None
        x, h = _residual(y, x, g_ffn_post, modv, l, FFN_GATE, nxt)

    y_prompt = x[:N_CTX_TOK].reshape(BATCH, SEQ, D_MODEL)
    y_sample = x[N_CTX_TOK:].reshape(DEC_BATCH, DEC_SEQ, D_MODEL)
    kv_shape = (BATCH, DEPTH, SEQ, NA_HEADS, NA_HEAD_DIM)
    new_k = jnp.stack([k.reshape(BATCH, SEQ, NA_WIDTH) for k in ks], axis=1).reshape(kv_shape)
    new_v = jnp.stack([v.reshape(BATCH, SEQ, NA_WIDTH) for v in vs], axis=1).reshape(kv_shape)
    new_rg = jnp.stack(rgs, axis=1)
    st = jnp.stack(s5s, axis=0).reshape(DEPTH, N_S5_CB, BATCH, 2, 2, S5_CB_GROUPS, S5_STATE)
    new_s5 = st.transpose(2, 0, 3, 1, 5, 6, 4).reshape(BATCH, DEPTH, 2, S5_GROUPS, S5_STATE, 2)
    return (y_prompt, y_sample, new_k, new_v, new_rg, new_s5)
```

```python
import functools
import math

import numpy as np
import jax
import jax.numpy as jnp
from jax import lax
from jax.experimental import pallas as pl
from jax.experimental.pallas import tpu as pltpu

F32 = jnp.float32
BF16 = jnp.bfloat16

D_MODEL = 2048
BATCH = 16
SEQ = 256
DEPTH = 4
DEC_BATCH = 4
DEC_SEQ = 1024
PAST_LEN = 512
GRID_W = 64
RG_WIDTH = 1024
RG_HEADS = 8
RG_BLOCK = RG_WIDTH // RG_HEADS
RG_CONV = 4
RG_C = 8.0
S5_WIDTH = 1024
S5_GROUP_CH = 16
S5_GROUPS = S5_WIDTH // S5_GROUP_CH
S5_STATE = 64
NA_HEADS = 8
NA_HEAD_DIM = 128
NA_WIDTH = NA_HEADS * NA_HEAD_DIM
NA_KH = 8
NA_KW = 16
NA_SCALE = NA_HEAD_DIM ** -0.5
NEG_INF = -1e30
N_BRANCH = 3
FF_HIDDEN = (8 * D_MODEL + 3 * 256 - 1) // (3 * 256) * 256
RMS_EPS = 1e-6
OFF_RG_X = 0
OFF_RG_G = OFF_RG_X + RG_WIDTH
OFF_S5 = OFF_RG_G + RG_WIDTH
OFF_Q = OFF_S5 + S5_WIDTH
OFF_K = OFF_Q + NA_WIDTH
OFF_V = OFF_K + NA_WIDTH
OFF_GATE = OFF_V + NA_WIDTH
IN_WIDTH = OFF_GATE + N_BRANCH * D_MODEL

N_CTX_TOK = BATCH * SEQ
N_LAT_TOK = DEC_BATCH * DEC_SEQ
N_TOK = N_CTX_TOK + N_LAT_TOK
N_COND = 8
GRID_ROWS = DEC_SEQ // GRID_W
S5_CB = 128
S5_CB_GROUPS = S5_CB // S5_GROUP_CH
S5_CB_STATE = S5_CB_GROUPS * S5_STATE
N_S5_CB = S5_WIDTH // S5_CB
SUBLANES = 8
S5_T = 4
S5_KC = S5_T * S5_CB
S5_ROWS = 256

VMEM_LIMIT = 52 * 1024 * 1024


def _cparams(n_grid):
    return pltpu.CompilerParams(dimension_semantics=("arbitrary",) * n_grid, vmem_limit_bytes=VMEM_LIMIT)


def _cond_row(start_row):
    return jnp.where(start_row < N_CTX_TOK, 0, 1 + (start_row - N_CTX_TOK) // DEC_SEQ)


def _mod_index(layer, k, start_row):
    return (layer * 6 + k) * N_COND + _cond_row(start_row)


def _rms(x):
    return x * lax.rsqrt(jnp.mean(x * x, axis=-1, keepdims=True) + RMS_EPS)


def _gelu_tanh(x):
    return 0.5 * x * (1.0 + jnp.tanh(math.sqrt(2.0 / math.pi) * (x + 0.044715 * (x * x * x))))


def _mod_kernel(c_ref, w_ref, b_ref, o_ref):
    cond = c_ref[...]
    a = (cond * jax.nn.sigmoid(cond)).astype(BF16)
    o_ref[...] = jnp.dot(a, w_ref[...].astype(BF16), preferred_element_type=F32) + b_ref[...]


def _modulation(cond, w_mod, b_mod):
    tn = 1024
    n_out = 6 * D_MODEL
    return pl.pallas_call(
        _mod_kernel,
        grid=(DEPTH, n_out // tn),
        in_specs=[
            pl.BlockSpec((N_COND, D_MODEL), lambda l, n: (0, 0)),
            pl.BlockSpec((None, D_MODEL, tn), lambda l, n: (l, 0, n)),
            pl.BlockSpec((None, 1, tn), lambda l, n: (l, 0, n)),
        ],
        out_specs=pl.BlockSpec((None, N_COND, tn), lambda l, n: (l, 0, n)),
        out_shape=jax.ShapeDtypeStruct((DEPTH, N_COND, n_out), F32),
        compiler_params=_cparams(2),
        name="modulation",
    )(cond, w_mod, b_mod.reshape(DEPTH, 1, n_out))


def _prenorm_kernel(x_ref, g_ref, sc_ref, sh_ref, o_ref):
    y = _rms(x_ref[...]) * g_ref[...]
    o_ref[...] = (y * (1.0 + sc_ref[...]) + sh_ref[...]).astype(BF16)


def _prenorm(x, g_all, modv, layer, k_shift, k_scale):
    tm = 512
    return pl.pallas_call(
        _prenorm_kernel,
        grid=(N_TOK // tm,),
        in_specs=[
            pl.BlockSpec((tm, D_MODEL), lambda i: (i, 0)),
            pl.BlockSpec((None, 1, D_MODEL), lambda i: (layer, 0, 0)),
            pl.BlockSpec((None, 1, D_MODEL), lambda i: (_mod_index(layer, k_scale, i * tm), 0, 0)),
            pl.BlockSpec((None, 1, D_MODEL), lambda i: (_mod_index(layer, k_shift, i * tm), 0, 0)),
        ],
        out_specs=pl.BlockSpec((tm, D_MODEL), lambda i: (i, 0)),
        out_shape=jax.ShapeDtypeStruct((N_TOK, D_MODEL), BF16),
        compiler_params=_cparams(1),
        name="prenorm",
    )(x, g_all, modv, modv)


def _proj_kernel(a_ref, w_ref, o_ref, wb_ref):
    @pl.when(pl.program_id(1) == 0)
    def _():
        wb_ref[...] = w_ref[...].astype(BF16)

    o_ref[...] = jnp.dot(a_ref[...], wb_ref[...], preferred_element_type=F32)


def _matmul(a, w, layer, *, tm, tn, name):
    k_dim, n_dim = w.shape[1], w.shape[2]
    return pl.pallas_call(
        _proj_kernel,
        grid=(n_dim // tn, N_TOK // tm),
        in_specs=[
            pl.BlockSpec((tm, k_dim), lambda n, m: (m, 0)),
            pl.BlockSpec((None, k_dim, tn), lambda n, m: (layer, 0, n)),
        ],
        out_specs=pl.BlockSpec((tm, tn), lambda n, m: (m, n)),
        out_shape=jax.ShapeDtypeStruct((N_TOK, n_dim), F32),
        scratch_shapes=[pltpu.VMEM((k_dim, tn), BF16)],
        compiler_params=_cparams(2),
        name=name,
    )(a, w)


def _tile_scan(a, b, carry, reverse):
    row = lax.broadcasted_iota(jnp.int32, a.shape, 0)
    for d in (1, 2, 4):
        if reverse:
            keep = row < SUBLANES - d
            shift = SUBLANES - d
        else:
            keep = row >= d
            shift = d
        sa = jnp.where(keep, pltpu.roll(a, shift, 0), 1.0)
        sb = jnp.where(keep, pltpu.roll(b, shift, 0), 0.0)
        b = a * sb + b
        a = a * sa
    h = a * carry + b
    return h, (h[0:1] if reverse else h[SUBLANES - 1:SUBLANES])


RG_ROWS = 128


def _rg_kernel(xr_ref, xg_ref, cw_ref, cb_ref, gw_ref, gb_ref, lam_ref, h0_ref, y_ref, st_ref,
               xp_ref, a_ref, b_ref, hf_ref, *, length, heads):
    pad = SUBLANES
    win = RG_ROWS + 2 * pad
    zeros = jnp.zeros((pad, heads * RG_BLOCK), F32)
    xp_ref[0:pad, :] = zeros
    xp_ref[pad + length:pad + length + pad, :] = zeros
    xp_ref[pad:pad + length, :] = xr_ref[...]
    neg = -lam_ref[...]
    softplus = jnp.maximum(neg, 0.0) + jnp.log1p(jnp.exp(-jnp.abs(neg)))

    def gates(c, _):
        r0 = pl.multiple_of(c * RG_ROWS, RG_ROWS)
        w = xp_ref[pl.ds(r0, win), :]
        tap = lambda s: pltpu.roll(w, s % win, 0)[pad:pad + RG_ROWS]
        x = (cw_ref[1:2, :] * w[pad:pad + RG_ROWS] + cb_ref[...] + cw_ref[0:1, :] * tap(1)
             + cw_ref[2:3, :] * tap(-1) + cw_ref[3:4, :] * tap(-2))
        xb = x.astype(BF16)
        for j in range(heads):
            cs = slice(j * RG_BLOCK, (j + 1) * RG_BLOCK)
            gl = jnp.dot(xb[:, cs], gw_ref[j], preferred_element_type=F32)
            for d in range(2):
                r = jax.nn.sigmoid(gl[:, (2 * d) * RG_BLOCK:(2 * d + 1) * RG_BLOCK] + gb_ref[2 * d:2 * d + 1, cs])
                i = jax.nn.sigmoid(gl[:, (2 * d + 1) * RG_BLOCK:(2 * d + 2) * RG_BLOCK]
                                   + gb_ref[2 * d + 1:2 * d + 2, cs])
                log_a = (-RG_C) * r * softplus[d:d + 1, cs]
                t = jnp.tanh(log_a)
                a_ref[d, pl.ds(r0, RG_ROWS), cs] = jnp.exp(log_a)
                b_ref[d, pl.ds(r0, RG_ROWS), cs] = jnp.sqrt(-2.0 * t / (1.0 - t)) * i * x[:, cs]
        return 0

    lax.fori_loop(0, length // RG_ROWS, gates, 0)
    n_tiles = length // SUBLANES

    def fwd(k, carry):
        r0 = pl.multiple_of(k * SUBLANES, SUBLANES)
        h, carry = _tile_scan(a_ref[0, pl.ds(r0, SUBLANES), :], b_ref[0, pl.ds(r0, SUBLANES), :], carry, False)
        hf_ref[pl.ds(r0, SUBLANES), :] = h
        return carry

    def bwd(k, carry):
        r0 = pl.multiple_of((n_tiles - 1 - k) * SUBLANES, SUBLANES)
        h, carry = _tile_scan(a_ref[1, pl.ds(r0, SUBLANES), :], b_ref[1, pl.ds(r0, SUBLANES), :], carry, True)
        y = (hf_ref[pl.ds(r0, SUBLANES), :] + h) * _gelu_tanh(xg_ref[pl.ds(r0, SUBLANES), :])
        y_ref[pl.ds(r0, SUBLANES), :] = y.astype(BF16)
        return carry

    st_ref[0:1, :] = lax.fori_loop(0, n_tiles, fwd, h0_ref[0:1, :])
    st_ref[1:2, :] = lax.fori_loop(0, n_tiles, bwd, h0_ref[1:2, :])


def _alias_kernel(body, n_in):
    def wrapped(*refs, **kw):
        return body(*refs[:n_in], *refs[n_in + 1:], **kw)
    return wrapped


def _rglru(proj, cw, cb, gw, gb, lam, h0, y_prev, *, length, n_seq, row0, heads):
    width = heads * RG_BLOCK
    n_cb = RG_WIDTH // width
    rb0 = row0 // length
    body = functools.partial(_rg_kernel, length=length, heads=heads)
    extra_specs, extra_args, aliases = [], [], {}
    if y_prev is not None:
        body = functools.partial(_alias_kernel(_rg_kernel, 8), length=length, heads=heads)
        extra_specs, extra_args, aliases = [pl.BlockSpec(memory_space=pl.ANY)], [y_prev], {8: 0}
    y, st = pl.pallas_call(
        body,
        grid=(n_seq, n_cb),
        input_output_aliases=aliases,
        in_specs=[
            pl.BlockSpec((length, width), lambda s, c: (rb0 + s, OFF_RG_X // width + c)),
            pl.BlockSpec((length, width), lambda s, c: (rb0 + s, OFF_RG_G // width + c)),
            pl.BlockSpec((RG_CONV, width), lambda s, c: (0, c)),
            pl.BlockSpec((1, width), lambda s, c: (0, c)),
            pl.BlockSpec((heads, RG_BLOCK, 4 * RG_BLOCK), lambda s, c: (c, 0, 0)),
            pl.BlockSpec((4, width), lambda s, c: (0, c)),
            pl.BlockSpec((2, width), lambda s, c: (0, c)),
            pl.BlockSpec((None, 2, width), lambda s, c: (s, 0, c)),
        ] + extra_specs,
        out_specs=[
            pl.BlockSpec((length, width), lambda s, c: (rb0 + s, c)),
            pl.BlockSpec((None, 2, width), lambda s, c: (s, 0, c)),
        ],
        out_shape=[
            jax.ShapeDtypeStruct((N_TOK, RG_WIDTH), BF16),
            jax.ShapeDtypeStruct((n_seq, 2, RG_WIDTH), F32),
        ],
        scratch_shapes=[
            pltpu.VMEM((length + 2 * SUBLANES, width), F32),
            pltpu.VMEM((2, length, width), F32),
            pltpu.VMEM((2, length, width), F32),
            pltpu.VMEM((length, width), F32),
        ],
        compiler_params=_cparams(2),
        name=f"rglru_{length}",
    )(proj, proj, cw, cb, gw, gb, lam, h0, *extra_args)
    return y, st


def _cmul(ar, ai, br, bi):
    return ar * br - ai * bi, ar * bi + ai * br


def _s5_kernel(u_ref, m_ref, g_ref, f_ref, lam_ref, d_ref, h0_ref, z_ref, st_ref, uc_ref, y_ref, e_ref,
               *, n_sub, sub_len, halves):
    ns = S5_CB_STATE
    n_c = sub_len // S5_T
    rows = n_sub * n_c
    for c in range(n_c):
        for t in range(S5_T):
            uc_ref[c * n_sub:(c + 1) * n_sub, t * S5_CB:(t + 1) * S5_CB] = (
                u_ref[pl.ds(c * S5_T + t, n_sub, stride=sub_len), :])

    def proj_in(i, _):
        r0 = pl.multiple_of(i * S5_ROWS, S5_ROWS)
        ub = uc_ref[pl.ds(r0, S5_ROWS), :].astype(BF16)
        y_ref[pl.ds(r0, S5_ROWS), :] = jnp.dot(ub, m_ref[...], preferred_element_type=F32)
        e_ref[pl.ds(r0, S5_ROWS), :] = jnp.dot(ub, g_ref[...], preferred_element_type=F32)
        return 0

    lax.fori_loop(0, rows // S5_ROWS, proj_in, 0)

    def run(d, init, store):
        cr = slice((2 * d) * ns, (2 * d + 1) * ns)
        ci = slice((2 * d + 1) * ns, (2 * d + 2) * ns)

        def body(k, s):
            c = k if d == 0 else n_c - 1 - k
            r0 = pl.multiple_of(c * n_sub, n_sub)
            er = e_ref[pl.ds(r0, n_sub), cr]
            ei = e_ref[pl.ds(r0, n_sub), ci]
            if store:
                e_ref[pl.ds(r0, n_sub), cr] = s[0]
                e_ref[pl.ds(r0, n_sub), ci] = s[1]
            nr, ni = _cmul(lam_ref[2 * d:2 * d + 1, :], lam_ref[2 * d + 1:2 * d + 2, :], s[0], s[1])
            return nr + er, ni + ei

        return lax.fori_loop(0, n_c, body, init)

    zero = jnp.zeros((n_sub, ns), F32)
    second = (lax.broadcasted_iota(jnp.int32, (n_sub, ns), 0) & 1) == 1
    for d in range(2):
        cr = slice((2 * d) * ns, (2 * d + 1) * ns)
        ci = slice((2 * d + 1) * ns, (2 * d + 2) * ns)
        init = (h0_ref[:, cr], h0_ref[:, ci])
        if halves == 2:
            er, ei = run(d, (zero, zero), False)
            pr, pi = lam_ref[2 * d:2 * d + 1, :], lam_ref[2 * d + 1:2 * d + 2, :]
            for _ in range(int(math.log2(n_c))):
                pr, pi = _cmul(pr, pi, pr, pi)
            hr, hi = _cmul(pr, pi, *init)
            shift = 1 if d == 0 else n_sub - 1
            hr, hi = hr + pltpu.roll(er, shift, 0), hi + pltpu.roll(ei, shift, 0)
            runs_first = jnp.logical_not(second) if d == 0 else second
            init = (jnp.where(runs_first, init[0], hr), jnp.where(runs_first, init[1], hi))
        sr, si = run(d, init, True)
        st_ref[:, cr] = sr
        st_ref[:, ci] = si

    def proj_out(i, _):
        r0 = pl.multiple_of(i * S5_ROWS, S5_ROWS)
        y = (y_ref[pl.ds(r0, S5_ROWS), :]
             + jnp.dot(e_ref[pl.ds(r0, S5_ROWS), :].astype(BF16), f_ref[...], preferred_element_type=F32)
             + d_ref[...] * uc_ref[pl.ds(r0, S5_ROWS), :])
        y_ref[pl.ds(r0, S5_ROWS), :] = _gelu_tanh(y)
        return 0

    lax.fori_loop(0, rows // S5_ROWS, proj_out, 0)
    for c in range(n_c):
        for t in range(S5_T):
            z_ref[pl.ds(c * S5_T + t, n_sub, stride=sub_len), :] = (
                y_ref[c * n_sub:(c + 1) * n_sub, t * S5_CB:(t + 1) * S5_CB])


def _s5(proj, tables, dvec, h0, z_prev, *, n_sub, sub_len, halves, group):
    mmat, gmat, fmat, lam_t = tables
    ns = S5_CB_STATE
    n_rows = n_sub * sub_len
    body = functools.partial(_s5_kernel, n_sub=n_sub, sub_len=sub_len, halves=halves)
    extra_specs, extra_args, aliases = [], [], {}
    if z_prev is not None:
        body = functools.partial(_alias_kernel(_s5_kernel, 7), n_sub=n_sub, sub_len=sub_len, halves=halves)
        extra_specs, extra_args, aliases = [pl.BlockSpec(memory_space=pl.ANY)], [z_prev], {7: 0}
    return pl.pallas_call(
        body,
        grid=(N_S5_CB,),
        input_output_aliases=aliases,
        in_specs=[
            pl.BlockSpec((n_rows, S5_CB), lambda c: (group, OFF_S5 // S5_CB + c)),
            pl.BlockSpec((None, S5_KC, S5_KC), lambda c: (c, 0, 0)),
            pl.BlockSpec((None, S5_KC, 4 * ns), lambda c: (c, 0, 0)),
            pl.BlockSpec((None, 4 * ns, S5_KC), lambda c: (c, 0, 0)),
            pl.BlockSpec((None, 4, ns), lambda c: (c, 0, 0)),
            pl.BlockSpec((None, 1, S5_KC), lambda c: (c, 0, 0)),
            pl.BlockSpec((None, n_sub, 4 * ns), lambda c: (c, 0, 0)),
        ] + extra_specs,
        out_specs=[
            pl.BlockSpec((n_rows, S5_CB), lambda c: (group, c)),
            pl.BlockSpec((None, n_sub, 4 * ns), lambda c: (c, 0, 0)),
        ],
        out_shape=[
            jax.ShapeDtypeStruct((N_TOK, S5_WIDTH), F32),
            jax.ShapeDtypeStruct((N_S5_CB, n_sub, 4 * ns), F32),
        ],
        scratch_shapes=[
            pltpu.VMEM((n_rows // S5_T, S5_KC), F32),
            pltpu.VMEM((n_rows // S5_T, S5_KC), F32),
            pltpu.VMEM((n_rows // S5_T, 4 * ns), F32),
        ],
        compiler_params=_cparams(1),
        name=f"s5_{sub_len}",
    )(proj, mmat, gmat, fmat, lam_t, dvec, h0, *extra_args)


def _s5_tables(lam_re, lam_im, log_dt, b_re, b_im, c_re, c_im):
    hp = lax.Precision.HIGHEST
    n_t = S5_T
    dt = jnp.exp(log_dt)[..., None]
    mag = jnp.exp(lam_re * dt)
    lr, li = mag * jnp.cos(lam_im * dt), mag * jnp.sin(lam_im * dt)
    den = lam_re * lam_re + lam_im * lam_im
    qr = ((lr - 1.0) * lam_re + li * lam_im) / den
    qi = (li * lam_re - (lr - 1.0) * lam_im) / den
    bbr = qr[..., None] * b_re - qi[..., None] * b_im
    bbi = qr[..., None] * b_im + qi[..., None] * b_re
    powers = [(jnp.ones_like(lr), jnp.zeros_like(lr))]
    for _ in range(n_t):
        powers.append(_cmul(*powers[-1], lr, li))
    pr = jnp.stack([p[0] for p in powers])
    pi = jnp.stack([p[1] for p in powers])
    lbr = pr[..., None] * bbr - pi[..., None] * bbi
    lbi = pr[..., None] * bbi + pi[..., None] * bbr
    wr = c_re * pr[:, :, :, None, :] - c_im * pi[:, :, :, None, :]
    wi = c_re * pi[:, :, :, None, :] + c_im * pr[:, :, :, None, :]
    kern = (jnp.einsum('kdgop,dgpi->kdgoi', wr, bbr, precision=hp)
            - jnp.einsum('kdgop,dgpi->kdgoi', wi, bbi, precision=hp))

    eye = jnp.eye(S5_CB_GROUPS, dtype=F32)
    blocks = (N_S5_CB, S5_CB_GROUPS)
    s_idx, t_idx = np.arange(n_t)[:, None], np.arange(n_t)[None, :]
    causal = (t_idx >= s_idx)[:, :, None, None, None]
    anti = (s_idx >= t_idx)[:, :, None, None, None]
    mst = (jnp.where(causal, kern[np.clip(t_idx - s_idx, 0, n_t), 0], 0.0)
           + jnp.where(anti, kern[np.clip(s_idx - t_idx, 0, n_t), 1], 0.0))
    mst = mst.reshape(n_t, n_t, *blocks, S5_GROUP_CH, S5_GROUP_CH)
    mmat = jnp.einsum('stkgoi,gh->ksgitho', mst, eye).astype(BF16).reshape(N_S5_CB, S5_KC, S5_KC)

    p_f, p_b = np.arange(n_t)[::-1], np.arange(n_t)
    gst = jnp.stack([jnp.stack([lbr[p_f, 0], lbi[p_f, 0]]), jnp.stack([lbr[p_b, 1], lbi[p_b, 1]])])
    gst = gst.reshape(2, 2, n_t, *blocks, S5_STATE, S5_GROUP_CH)
    gmat = jnp.einsum('dcskgpi,gh->ksgidchp', gst, eye).astype(BF16).reshape(N_S5_CB, S5_KC, 4 * S5_CB_STATE)

    q_f, q_b = np.arange(n_t) + 1, n_t - np.arange(n_t)
    fst = jnp.stack([jnp.stack([wr[q_f, 0], -wi[q_f, 0]]), jnp.stack([wr[q_b, 1], -wi[q_b, 1]])])
    fst = fst.reshape(2, 2, n_t, *blocks, S5_GROUP_CH, S5_STATE)
    fmat = jnp.einsum('dctkgop,gh->kdchptgo', fst, eye).astype(BF16).reshape(N_S5_CB, 4 * S5_CB_STATE, S5_KC)

    lam_t = jnp.stack([pr[n_t], pi[n_t]], axis=1)
    lam_t = lam_t.reshape(2, 2, N_S5_CB, S5_CB_STATE).transpose(2, 0, 1, 3).reshape(N_S5_CB, 4, S5_CB_STATE)
    return mmat, gmat, fmat, lam_t


def _s5_glu_kernel(z_ref, w_ref, o_ref):
    z = z_ref[...]
    g = jnp.dot(z.astype(BF16), w_ref[...].astype(BF16), preferred_element_type=F32)
    o_ref[...] = (z * jax.nn.sigmoid(g)).astype(BF16)


def _s5_glu(z, w_glu, layer):
    tm = 1024
    return pl.pallas_call(
        _s5_glu_kernel,
        grid=(N_TOK // tm,),
        in_specs=[
            pl.BlockSpec((tm, S5_WIDTH), lambda i: (i, 0)),
            pl.BlockSpec((None, S5_WIDTH, S5_WIDTH), lambda i: (layer, 0, 0)),
        ],
        out_specs=pl.BlockSpec((tm, S5_WIDTH), lambda i: (i, 0)),
        out_shape=jax.ShapeDtypeStruct((N_TOK, S5_WIDTH), BF16),
        compiler_params=_cparams(1),
        name="s5_glu",
    )(z, w_glu)


def _dot_nt(a, b):
    return lax.dot_general(a, b, (((1,), (1,)), ((), ())), preferred_element_type=F32)


def _ctx_attn_kernel(q_ref, k_ref, v_ref, o_ref):
    for h in range(NA_HEADS):
        cs = slice(h * NA_HEAD_DIM, (h + 1) * NA_HEAD_DIM)
        q = (q_ref[:, cs] * NA_SCALE).astype(BF16)
        s = _dot_nt(q, k_ref[:, cs].astype(BF16))
        p = jnp.exp(s - jnp.max(s, axis=-1, keepdims=True))
        p = (p / jnp.sum(p, axis=-1, keepdims=True)).astype(BF16)
        o_ref[:, cs] = jnp.dot(p, v_ref[:, cs].astype(BF16), preferred_element_type=F32).astype(BF16)


def _ctx_attention(proj):
    spec = lambda off: pl.BlockSpec((SEQ, NA_WIDTH), lambda b: (b, off // NA_WIDTH))
    return pl.pallas_call(
        _ctx_attn_kernel,
        grid=(BATCH,),
        in_specs=[spec(OFF_Q), spec(OFF_K), spec(OFF_V)],
        out_specs=pl.BlockSpec((SEQ, NA_WIDTH), lambda b: (b, 0)),
        out_shape=jax.ShapeDtypeStruct((N_TOK, NA_WIDTH), BF16),
        compiler_params=_cparams(1),
        name="ctx_attention",
    )(proj, proj, proj)


def _na_row_start(r):
    return min(max(r - NA_KH // 2, 0), GRID_ROWS - NA_KH)


def _na_variant(r):
    lo, hi = NA_KH // 2, GRID_ROWS - NA_KH // 2
    if r < lo:
        return r
    if r <= hi:
        return lo
    return r - hi + lo


N_NA_VARIANTS = _na_variant(GRID_ROWS - 1) + 1
NA_LOCAL = NA_KH * GRID_W


def _na_kernel(q_ref, k_ref, v_ref, kc_ref, vc_ref, bias_ref, o_ref):
    k = k_ref[...].astype(BF16)
    v = v_ref[...].astype(BF16)
    kc = kc_ref[...].astype(BF16)
    vc = vc_ref[...].astype(BF16)
    for r in range(GRID_ROWS):
        k0 = _na_row_start(r) * GRID_W
        q = (q_ref[r * GRID_W:(r + 1) * GRID_W, :] * NA_SCALE).astype(BF16)
        s_loc = _dot_nt(q, k[k0:k0 + NA_LOCAL]) + bias_ref[_na_variant(r)]
        s_ctx = _dot_nt(q, kc)
        m = jnp.maximum(jnp.max(s_loc, axis=-1, keepdims=True), jnp.max(s_ctx, axis=-1, keepdims=True))
        p_loc = jnp.exp(s_loc - m)
        p_ctx = jnp.exp(s_ctx - m)
        inv = 1.0 / (jnp.sum(p_loc, axis=-1, keepdims=True) + jnp.sum(p_ctx, axis=-1, keepdims=True))
        o = (jnp.dot((p_loc * inv).astype(BF16), v[k0:k0 + NA_LOCAL], preferred_element_type=F32)
             + jnp.dot((p_ctx * inv).astype(BF16), vc, preferred_element_type=F32))
        o_ref[r * GRID_W:(r + 1) * GRID_W, :] = o.astype(BF16)


def _na_attention(proj, cache_k, cache_v, bias, y_ctx, layer):
    rb0 = N_CTX_TOK // DEC_SEQ
    spec = lambda off: pl.BlockSpec((DEC_SEQ, NA_HEAD_DIM), lambda b, h: (rb0 + b, off // NA_HEAD_DIM + h))
    cspec = pl.BlockSpec((None, None, PAST_LEN, NA_HEAD_DIM), lambda b, h: (b, layer, 0, h))
    return pl.pallas_call(
        _alias_kernel(_na_kernel, 6),
        grid=(DEC_BATCH, NA_HEADS),
        input_output_aliases={6: 0},
        in_specs=[spec(OFF_Q), spec(OFF_K), spec(OFF_V), cspec, cspec,
                  pl.BlockSpec((None, None, N_NA_VARIANTS, GRID_W, NA_LOCAL), lambda b, h: (layer, h, 0, 0, 0)),
                  pl.BlockSpec(memory_space=pl.ANY)],
        out_specs=pl.BlockSpec((DEC_SEQ, NA_HEAD_DIM), lambda b, h: (rb0 + b, h)),
        out_shape=jax.ShapeDtypeStruct((N_TOK, NA_WIDTH), BF16),
        compiler_params=_cparams(2),
        name="na_attention",
    )(proj, proj, proj, cache_k, cache_v, bias, y_ctx)


def _na_bias_tables(na_rpb):
    qc = np.arange(GRID_W)[:, None]
    kc = np.arange(GRID_W)[None, :]
    win = np.clip(qc - NA_KW // 2, 0, GRID_W - NA_KW)
    in_win = (kc >= win) & (kc < win + NA_KW)
    col = np.clip(kc - qc, -(NA_KW - 1), NA_KW - 1) + NA_KW - 1
    toep = jnp.where(in_win, na_rpb[:, :, :, col], NEG_INF)
    rows = [next(r for r in range(GRID_ROWS) if _na_variant(r) == v) for v in range(N_NA_VARIANTS)]
    ridx = np.array([[_na_row_start(r) + i - r + NA_KH - 1 for i in range(NA_KH)] for r in rows])
    tab = toep[:, :, ridx]
    return tab.transpose(0, 1, 2, 4, 3, 5).reshape(DEPTH, NA_HEADS, N_NA_VARIANTS, GRID_W, NA_LOCAL)


def _merge_kernel(y0_ref, y1_ref, y2_ref, w0_ref, w1_ref, w2_ref, g0_ref, g1_ref, g2_ref, o_ref, wb_ref):
    @pl.when(pl.program_id(1) == 0)
    def _():
        wb_ref[0] = w0_ref[...].astype(BF16)
        wb_ref[1] = w1_ref[...].astype(BF16)
        wb_ref[2] = w2_ref[...].astype(BF16)

    m = jax.nn.sigmoid(g0_ref[...]) * jnp.dot(y0_ref[...], wb_ref[0], preferred_element_type=F32)
    m = m + jax.nn.sigmoid(g1_ref[...]) * jnp.dot(y1_ref[...], wb_ref[1], preferred_element_type=F32)
    m = m + jax.nn.sigmoid(g2_ref[...]) * jnp.dot(y2_ref[...], wb_ref[2], preferred_element_type=F32)
    o_ref[...] = m.astype(BF16)


def _merge(y_rg, y_s5, y_na, w_rg, w_s5, w_na, proj, layer):
    tm, tn = 256, 1024
    width = RG_WIDTH
    yspec = pl.BlockSpec((tm, width), lambda n, m: (m, 0))
    wspec = pl.BlockSpec((None, width, tn), lambda n, m: (layer, 0, n))
    gspec = lambda b: pl.BlockSpec((tm, tn), lambda n, m: (m, (OFF_GATE + b * D_MODEL) // tn + n))
    return pl.pallas_call(
        _merge_kernel,
        grid=(D_MODEL // tn, N_TOK // tm),
        in_specs=[yspec, yspec, yspec, wspec, wspec, wspec, gspec(0), gspec(1), gspec(2)],
        out_specs=pl.BlockSpec((tm, tn), lambda n, m: (m, n)),
        out_shape=jax.ShapeDtypeStruct((N_TOK, D_MODEL), BF16),
        scratch_shapes=[pltpu.VMEM((N_BRANCH, width, tn), BF16)],
        compiler_params=_cparams(2),
        name="merge",
    )(y_rg, y_s5, y_na, w_rg, w_s5, w_na, proj, proj, proj)


def _res_kernel(*refs, emit_next):
    if emit_next:
        y_ref, x_ref, gp_ref, gate_ref, gn_ref, sc_ref, sh_ref, xo_ref, ho_ref = refs
    else:
        y_ref, x_ref, gp_ref, gate_ref, xo_ref = refs
    xn = x_ref[...] + gate_ref[...] * (_rms(y_ref[...]) * gp_ref[...])
    xo_ref[...] = xn
    if emit_next:
        ho_ref[...] = ((_rms(xn) * gn_ref[...]) * (1.0 + sc_ref[...]) + sh_ref[...]).astype(BF16)


def _residual(y, x, g_post, modv, layer, k_gate, nxt):
    tm = 512
    row = lambda lyr: pl.BlockSpec((None, 1, D_MODEL), lambda i: (lyr, 0, 0))
    modrow = lambda lyr, kk: pl.BlockSpec((None, 1, D_MODEL), lambda i: (_mod_index(lyr, kk, i * tm), 0, 0))
    xspec = pl.BlockSpec((tm, D_MODEL), lambda i: (i, 0))
    in_specs = [xspec, xspec, row(layer), modrow(layer, k_gate)]
    args = [y, x, g_post, modv]
    out_specs = [xspec]
    out_shape = [jax.ShapeDtypeStruct((N_TOK, D_MODEL), F32)]
    if nxt is not None:
        g_next, l_next, k_shift, k_scale = nxt
        in_specs += [row(l_next), modrow(l_next, k_scale), modrow(l_next, k_shift)]
        args += [g_next, modv, modv]
        out_specs.append(xspec)
        out_shape.append(jax.ShapeDtypeStruct((N_TOK, D_MODEL), BF16))
    out = pl.pallas_call(
        functools.partial(_res_kernel, emit_next=nxt is not None),
        grid=(N_TOK // tm,),
        in_specs=in_specs,
        out_specs=out_specs,
        out_shape=out_shape,
        compiler_params=_cparams(1),
        name="residual",
    )(*args)
    return (out[0], out[1]) if nxt is not None else (out[0], None)


def _ffn_in_kernel(h_ref, wa_ref, wb_ref, o_ref, wab_ref, wbb_ref):
    @pl.when(pl.program_id(1) == 0)
    def _():
        wab_ref[...] = wa_ref[...].astype(BF16)
        wbb_ref[...] = wb_ref[...].astype(BF16)

    h = h_ref[...]
    a = jnp.dot(h, wab_ref[...], preferred_element_type=F32)
    b = jnp.dot(h, wbb_ref[...], preferred_element_type=F32)
    o_ref[...] = (a * jax.nn.sigmoid(a) * b).astype(BF16)


def _ffn_in(h, w_ffn_in, layer):
    tm, tn = 1024, 512
    n_n = FF_HIDDEN // tn
    return pl.pallas_call(
        _ffn_in_kernel,
        grid=(n_n, N_TOK // tm),
        in_specs=[
            pl.BlockSpec((tm, D_MODEL), lambda n, m: (m, 0)),
            pl.BlockSpec((None, D_MODEL, tn), lambda n, m: (layer, 0, n)),
            pl.BlockSpec((None, D_MODEL, tn), lambda n, m: (layer, 0, n_n + n)),
        ],
        out_specs=pl.BlockSpec((tm, tn), lambda n, m: (m, n)),
        out_shape=jax.ShapeDtypeStruct((N_TOK, FF_HIDDEN), BF16),
        scratch_shapes=[pltpu.VMEM((D_MODEL, tn), BF16), pltpu.VMEM((D_MODEL, tn), BF16)],
        compiler_params=_cparams(2),
        name="ffn_in",
    )(h, w_ffn_in, w_ffn_in)


MIX_SHIFT, MIX_SCALE, MIX_GATE, FFN_SHIFT, FFN_SCALE, FFN_GATE = range(6)


def kernel(x_prompt, x_sample, cache_na_k, cache_na_v, state_rglru, state_s5, c, c_ctx, w_mod, b_mod, g_mix_pre, g_mix_post, g_ffn_pre, g_ffn_post, w_in, rg_conv_w, rg_conv_b, rg_gate_w, rg_gate_b, rg_lambda, s5_lambda_re, s5_lambda_im, s5_log_dt, s5_b_re, s5_b_im, s5_c_re, s5_c_im, s5_d, s5_w_glu, na_rpb, w_rg_out, w_s5_out, w_na_out, w_o, w_ffn_in, w_ffn_out):
    x = jnp.concatenate([x_prompt.reshape(N_CTX_TOK, D_MODEL), x_sample.reshape(N_LAT_TOK, D_MODEL)], axis=0)
    cond = jnp.concatenate([c_ctx[None, :], c, jnp.zeros((N_COND - 1 - DEC_BATCH, D_MODEL), F32)], axis=0)
    mods = _modulation(cond, w_mod, b_mod)
    modv = mods.reshape(DEPTH, N_COND, 6, D_MODEL).transpose(0, 2, 1, 3).reshape(DEPTH * 6 * N_COND, 1, D_MODEL)

    row3 = lambda g: g.reshape(DEPTH, 1, D_MODEL)
    g_mix_pre, g_mix_post, g_ffn_pre, g_ffn_post = map(row3, (g_mix_pre, g_mix_post, g_ffn_pre, g_ffn_post))
    gate_w = rg_gate_w.transpose(0, 3, 4, 1, 2, 5).reshape(DEPTH, RG_HEADS, RG_BLOCK, 4 * RG_BLOCK).astype(BF16)
    gate_b = rg_gate_b.reshape(DEPTH, 4, RG_WIDTH)
    cache_k = cache_na_k.reshape(DEC_BATCH, DEPTH, PAST_LEN, NA_WIDTH)
    cache_v = cache_na_v.reshape(DEC_BATCH, DEPTH, PAST_LEN, NA_WIDTH)
    na_bias = _na_bias_tables(na_rpb)
    s5_h0 = state_s5.reshape(DEC_BATCH, DEPTH, 2, N_S5_CB, S5_CB_STATE, 2).transpose(1, 3, 0, 2, 5, 4)
    s5_h0 = jnp.repeat(s5_h0.reshape(DEPTH, N_S5_CB, DEC_BATCH, 4 * S5_CB_STATE), 2, axis=2)
    rg_zero = jnp.zeros((BATCH, 2, RG_WIDTH), F32)
    s5_zero = jnp.zeros((N_S5_CB, BATCH, 4 * S5_CB_STATE), F32)
    s5_skip = jnp.tile(s5_d.reshape(DEPTH, N_S5_CB, 1, S5_CB), (1, 1, 1, S5_T))

    h = _prenorm(x, g_mix_pre, modv, 0, MIX_SHIFT, MIX_SCALE)
    ks, vs, rgs, s5s = [], [], [], []
    for l in range(DEPTH):
        proj = _matmul(h, w_in, l, tm=1024, tn=1024, name="in_proj")
        ks.append(proj[:N_CTX_TOK, OFF_K:OFF_V])
        vs.append(proj[:N_CTX_TOK, OFF_V:OFF_GATE])

        rg_args = (rg_conv_w[l], rg_conv_b[l][None, :], gate_w[l], gate_b[l], rg_lambda[l])
        y_rg, st_rg = _rglru(proj, *rg_args, rg_zero, None, length=SEQ, n_seq=BATCH, row0=0, heads=8)
        y_rg, _ = _rglru(proj, *rg_args, state_rglru[:, l], y_rg, length=DEC_SEQ, n_seq=DEC_BATCH, row0=N_CTX_TOK, heads=4)
        rgs.append(st_rg)

        tables = _s5_tables(s5_lambda_re[l], s5_lambda_im[l], s5_log_dt[l], s5_b_re[l], s5_b_im[l],
                            s5_c_re[l], s5_c_im[l])
        z, st_s5 = _s5(proj, tables, s5_skip[l], s5_zero, None, n_sub=BATCH, sub_len=SEQ, halves=1, group=0)
        z, _ = _s5(proj, tables, s5_skip[l], s5_h0[l], z, n_sub=2 * DEC_BATCH, sub_len=DEC_SEQ // 2, halves=2, group=1)
        s5s.append(st_s5)
        y_s5 = _s5_glu(z, s5_w_glu, l)

        y_na = _na_attention(proj, cache_k, cache_v, na_bias, _ctx_attention(proj), l)

        m = _merge(y_rg, y_s5, y_na, w_rg_out, w_s5_out, w_na_out, proj, l)
        y = _matmul(m, w_o, l, tm=1024, tn=1024, name="out_proj")
        x, h = _residual(y, x, g_mix_post, modv, l, MIX_GATE, (g_ffn_pre, l, FFN_SHIFT, FFN_SCALE))
        f = _ffn_in(h, w_ffn_in, l)
        y = _matmul(f, w_ffn_out, l, tm=512, tn=512, name="ffn_out")
        nxt = (g_mix_pre, l + 1, MIX_SHIFT, MIX_SCALE) if l + 1 < DEPTH else None
        x, h = _residual(y, x, g_ffn_post, modv, l, FFN_GATE, nxt)

    y_prompt = x[:N_CTX_TOK].reshape(BATCH, SEQ, D_MODEL)
    y_sample = x[N_CTX_TOK:].reshape(DEC_BATCH, DEC_SEQ, D_MODEL)
    kv_shape = (BATCH, DEPTH, SEQ, NA_HEADS, NA_HEAD_DIM)
    new_k = jnp.stack([k.reshape(BATCH, SEQ, NA_WIDTH) for k in ks], axis=1).reshape(kv_shape)
    new_v = jnp.stack([v.reshape(BATCH, SEQ, NA_WIDTH) for v in vs], axis=1).reshape(kv_shape)
    new_rg = jnp.stack(rgs, axis=1)
    st = jnp.stack(s5s, axis=0).reshape(DEPTH, N_S5_CB, BATCH, 2, 2, S5_CB_GROUPS, S5_STATE)
    new_s5 = st.transpose(2, 0, 3, 1, 5, 6, 4).reshape(BATCH, DEPTH, 2, S5_GROUPS, S5_STATE, 2)
    return (y_prompt, y_sample, new_k, new_v, new_rg, new_s5)
```
